```python
import jax, jax.numpy as jnp
from jax import lax
import numpy as np

D_MODEL = 2048
BATCH = 4
SEQ = 2048
DEPTH = 4

CHUNK = 64
A_HEAD_DIM = 128
A_WIDTH = D_MODEL // 2
A_HEADS = A_WIDTH // A_HEAD_DIM
A_LEFT_CHUNKS = 8
A_BAND = A_LEFT_CHUNKS + 1
A_MAX_REL = 256
B_WIDTH = D_MODEL - A_WIDTH
CONV_WIDTH = 3
C_BLOCK = 128
C_WIDTH = D_MODEL
C_GROUPS = 8
FFN_HIDDEN = -(-8 * D_MODEL // (3 * 256)) * 256
N_EVEN = (DEPTH + 1) // 2
N_ODD = DEPTH // 2
EPS = 1e-6
NEG_INF = -1e30

kernel_name = 'hybrid_chunk_attn_conv_gmlp_trunk'


def rms_norm(x, g):
    x32 = x.astype(jnp.float32)
    y = x32 * lax.rsqrt(jnp.mean(x32 * x32, axis=-1, keepdims=True) + EPS)
    return (y * g.astype(jnp.float32)).astype(x.dtype)


def layer_norm(x, g, b):
    x32 = x.astype(jnp.float32)
    mu = jnp.mean(x32, axis=-1, keepdims=True)
    xc = x32 - mu
    var = jnp.mean(xc * xc, axis=-1, keepdims=True)
    y = xc * lax.rsqrt(var + EPS) * g.astype(jnp.float32) + b.astype(jnp.float32)
    return y.astype(x.dtype)


def band_attention(q, k, v, rel_bias):
    bsz, seq, heads, dh = q.shape
    nc = seq // CHUNK
    qc = q.reshape(bsz, nc, CHUNK, heads, dh)
    pad = ((0, 0), (A_LEFT_CHUNKS * CHUNK, 0), (0, 0), (0, 0))
    kp = jnp.pad(k, pad).reshape(bsz, nc + A_LEFT_CHUNKS, CHUNK, heads, dh)
    vp = jnp.pad(v, pad).reshape(bsz, nc + A_LEFT_CHUNKS, CHUNK, heads, dh)
    band_idx = jnp.arange(nc)[:, None] + jnp.arange(A_BAND)[None, :]
    kb = kp[:, band_idx].reshape(bsz, nc, A_BAND * CHUNK, heads, dh)
    vb = vp[:, band_idx].reshape(bsz, nc, A_BAND * CHUNK, heads, dh)
    scores = jnp.einsum('bnqhd,bnkhd->bhnqk', qc, kb).astype(jnp.float32) * (dh ** -0.5)
    q_pos = jnp.arange(CHUNK)[:, None] + A_LEFT_CHUNKS * CHUNK
    k_pos = jnp.arange(A_BAND * CHUNK)[None, :]
    rel_idx = jnp.clip(q_pos - k_pos, -A_MAX_REL, A_MAX_REL) + A_MAX_REL
    bias = rel_bias.astype(jnp.float32)[:, rel_idx]
    scores = scores + bias[:, None]
    valid = jnp.repeat(band_idx >= A_LEFT_CHUNKS, CHUNK, axis=1)
    scores = jnp.where(valid[None, None, :, None, :], scores, NEG_INF)
    p = jax.nn.softmax(scores, axis=-1).astype(v.dtype)
    out = jnp.einsum('bhnqk,bnkhd->bnqhd', p, vb)
    return out.reshape(bsz, seq, heads * dh)


def gated_short_conv(b_gate, c_gate, h, conv_w):
    z = c_gate * h
    seq = z.shape[1]
    zp = jnp.pad(z, ((0, 0), (CONV_WIDTH - 1, 0), (0, 0)))
    y = conv_w[0] * zp[:, 0:seq]
    for j in range(1, CONV_WIDTH):
        y = y + conv_w[j] * zp[:, j:j + seq]
    return b_gate * y


def mixer_ab(h, w_in, rel_bias, conv_w, w_out):
    bsz, seq, _ = h.shape
    proj = h @ w_in
    cuts = [A_WIDTH, 2 * A_WIDTH, 3 * A_WIDTH, 3 * A_WIDTH + B_WIDTH, 3 * A_WIDTH + 2 * B_WIDTH]
    q, k, v, b_gate, c_gate, hv = jnp.split(proj, cuts, axis=-1)
    shp = (bsz, seq, A_HEADS, A_HEAD_DIM)
    attn = band_attention(q.reshape(shp), k.reshape(shp), v.reshape(shp), rel_bias)
    conv = gated_short_conv(b_gate, c_gate, hv, conv_w)
    return jnp.concatenate([attn, conv], axis=-1) @ w_out


def mixer_c(h, w_in, ln_g, ln_b, w_s, b_s, w_out):
    bsz, seq, _ = h.shape
    nb = seq // C_BLOCK
    z = jax.nn.gelu(h @ w_in, approximate=False)
    u, v = jnp.split(z, 2, axis=-1)
    v = layer_norm(v, ln_g, ln_b)
    pos = jnp.arange(C_BLOCK)
    mask = (pos[None, :] // CHUNK) <= (pos[:, None] // CHUNK)
    w_m = jnp.where(mask[None], w_s, jnp.zeros_like(w_s))
    vg = v.reshape(bsz, nb, C_BLOCK, C_GROUPS, C_WIDTH // C_GROUPS)
    s = jnp.einsum('gts,bnsgd->bntgd', w_m, vg) + jnp.transpose(b_s)[:, :, None]
    return (u * s.reshape(bsz, seq, C_WIDTH)) @ w_out


def swiglu(h, w_gate, w_up, w_down):
    return (jax.nn.silu(h @ w_gate) * (h @ w_up)) @ w_down


def setup_inputs(seed: int = 0) -> dict:
    key = jax.random.key(seed)
    ks = jax.random.split(key, 17)
    nrm = jax.random.normal
    f32 = jnp.float32
    in_ab = 3 * A_WIDTH + 3 * B_WIDTH
    return {
        'x': nrm(ks[0], (BATCH, SEQ, D_MODEL), f32),
        'mix_norm': 1.0 + 0.02 * nrm(ks[1], (DEPTH, D_MODEL), f32),
        'ab_w_in': nrm(ks[2], (N_EVEN, D_MODEL, in_ab), f32) * D_MODEL ** -0.5,
        'ab_rel_bias': 0.5 * nrm(ks[3], (N_EVEN, A_HEADS, 2 * A_MAX_REL + 1), f32),
        'ab_conv_w': nrm(ks[4], (N_EVEN, CONV_WIDTH, B_WIDTH), f32) * CONV_WIDTH ** -0.5,
        'ab_w_out': nrm(ks[5], (N_EVEN, A_WIDTH + B_WIDTH, D_MODEL), f32) * (A_WIDTH + B_WIDTH) ** -0.5,
        'c_w_in': nrm(ks[6], (N_ODD, D_MODEL, 2 * C_WIDTH), f32) * D_MODEL ** -0.5,
        'c_ln_g': 1.0 + 0.02 * nrm(ks[7], (N_ODD, C_WIDTH), f32),
        'c_ln_b': 0.02 * nrm(ks[8], (N_ODD, C_WIDTH), f32),
        'c_w_s': nrm(ks[9], (N_ODD, C_GROUPS, C_BLOCK, C_BLOCK), f32) * C_BLOCK ** -0.5,
        'c_b_s': 1.0 + 0.02 * nrm(ks[10], (N_ODD, C_GROUPS, C_BLOCK), f32),
        'c_w_out': nrm(ks[11], (N_ODD, C_WIDTH, D_MODEL), f32) * C_WIDTH ** -0.5,
        'ffn_norm': 1.0 + 0.02 * nrm(ks[12], (DEPTH, D_MODEL), f32),
        'ffn_w_gate': nrm(ks[13], (DEPTH, D_MODEL, FFN_HIDDEN), f32) * D_MODEL ** -0.5,
        'ffn_w_up': nrm(ks[14], (DEPTH, D_MODEL, FFN_HIDDEN), f32) * D_MODEL ** -0.5,
        'ffn_w_down': nrm(ks[15], (DEPTH, FFN_HIDDEN, D_MODEL), f32) * FFN_HIDDEN ** -0.5,
        'final_norm': 1.0 + 0.02 * nrm(ks[16], (D_MODEL,), f32),
    }


def reference(x, mix_norm, ab_w_in, ab_rel_bias, ab_conv_w, ab_w_out, c_w_in, c_ln_g, c_ln_b,
              c_w_s, c_b_s, c_w_out, ffn_norm, ffn_w_gate, ffn_w_up, ffn_w_down, final_norm):
    for layer in range(DEPTH):
        i = layer // 2
        h = rms_norm(x, mix_norm[layer])
        if layer % 2 == 0:
            x = x + mixer_ab(h, ab_w_in[i], ab_rel_bias[i], ab_conv_w[i], ab_w_out[i])
        else:
            x = x + mixer_c(h, c_w_in[i], c_ln_g[i], c_ln_b[i], c_w_s[i], c_b_s[i], c_w_out[i])
        h = rms_norm(x, ffn_norm[layer])
        x = x + swiglu(h, ffn_w_gate[layer], ffn_w_up[layer], ffn_w_down[layer])
    return rms_norm(x, final_norm)
```

```python
import functools

import jax
import jax.numpy as jnp
import numpy as np
from jax import lax
from jax.experimental import pallas as pl
from jax.experimental.pallas import tpu as pltpu

CHUNK = 64
A_HEAD_DIM = 128
A_LEFT_CHUNKS = 8
A_MAX_REL = 256
CONV_WIDTH = 3
C_BLOCK = 128
C_GROUPS = 8
EPS = 1e-6
NEG_INF = -1e30

V7X_VMEM_BYTES = 64 * 1024 * 1024
V7X_SUBLANES = 8

ROW_TILE_PROJ = 1024
ROW_TILE_MIX = 512
COL_TILE_PROJ = 1024
COL_TILE_FFN = 512
Q_BLOCK = 8 * CHUNK
K_WINDOW = Q_BLOCK + A_LEFT_CHUNKS * CHUNK

_SQRT_HALF = np.sqrt(0.5).astype(np.float32)


def _vmem_limit(pipelined_bytes, scratch_bytes, temp_bytes):
    need = 2 * pipelined_bytes + scratch_bytes + temp_bytes
    return int(min(need + need // 8, V7X_VMEM_BYTES - 4 * 1024 * 1024))


def _nbytes(shape, dtype):
    return int(np.prod(shape)) * jnp.dtype(dtype).itemsize


def _rms_norm(x, g):
    ms = jnp.mean(x * x, axis=-1, keepdims=True)
    return (x * lax.rsqrt(ms + EPS)) * g


def _norm_proj_kernel(x_ref, g_ref, w_ref, o_ref, h_ref, *, gelu):
    @pl.when(pl.program_id(1) == 0)
    def _():
        h_ref[...] = _rms_norm(x_ref[...], g_ref[...]).astype(h_ref.dtype)

    y = jnp.dot(h_ref[...], w_ref[...], preferred_element_type=jnp.float32)
    if gelu:
        y = 0.5 * y * (1.0 + lax.erf(y * _SQRT_HALF))
    o_ref[...] = y.astype(o_ref.dtype)


def _norm_proj(x, g, w, *, col_start, n_cols, out_dtype, gelu, name):
    m, d = x.shape
    tm, tn = ROW_TILE_PROJ, COL_TILE_PROJ
    assert m % tm == 0 and n_cols % tn == 0 and col_start % tn == 0
    col0 = col_start // tn
    pipelined = (_nbytes((tm, d), x.dtype) + _nbytes((d, tn), w.dtype)
                 + _nbytes((tm, tn), out_dtype))
    scratch = _nbytes((tm, d), jnp.bfloat16)
    temps = 3 * _nbytes((tm, tn), jnp.float32)
    return pl.pallas_call(
        functools.partial(_norm_proj_kernel, gelu=gelu),
        grid=(m // tm, n_cols // tn),
        in_specs=[
            pl.BlockSpec((tm, d), lambda i, j: (i, 0)),
            pl.BlockSpec((1, d), lambda i, j: (0, 0)),
            pl.BlockSpec((d, tn), lambda i, j: (0, col0 + j)),
        ],
        out_specs=pl.BlockSpec((tm, tn), lambda i, j: (i, j)),
        out_shape=jax.ShapeDtypeStruct((m, n_cols), out_dtype),
        scratch_shapes=[pltpu.VMEM((tm, d), jnp.bfloat16)],
        compiler_params=pltpu.CompilerParams(
            dimension_semantics=("parallel", "arbitrary"),
            vmem_limit_bytes=_vmem_limit(pipelined, scratch, temps)),
        name=name,
    )(x, g.reshape(1, d), w)


def _bias_table_kernel(g_ref, o_ref):
    width = Q_BLOCK + K_WINDOW
    x = jnp.broadcast_to(g_ref[0], (Q_BLOCK, width))
    row = lax.broadcasted_iota(jnp.int32, (Q_BLOCK, width), 0)
    shift = 1
    while shift < Q_BLOCK:
        x = jnp.where((row & shift) != 0, pltpu.roll(x, shift, 1), x)
        shift *= 2
    t = x[:, Q_BLOCK:]
    q_chunk = lax.broadcasted_iota(jnp.int32, (Q_BLOCK, K_WINDOW), 0) // CHUNK
    k_chunk = lax.broadcasted_iota(jnp.int32, (Q_BLOCK, K_WINDOW), 1) // CHUNK
    band = k_chunk - q_chunk
    o_ref[0] = jnp.where((band >= 0) & (band <= A_LEFT_CHUNKS), t, NEG_INF)


def _bias_table(rel_bias):
    heads = rel_bias.shape[0]
    width = Q_BLOCK + K_WINDOW
    left = Q_BLOCK + A_LEFT_CHUNKS * CHUNK - A_MAX_REL
    right = width - left - (2 * A_MAX_REL + 1)
    g = jnp.pad(rel_bias[:, ::-1], ((0, 0), (left, right)), mode="edge")
    return pl.pallas_call(
        _bias_table_kernel,
        grid=(heads,),
        in_specs=[pl.BlockSpec((1, 1, width), lambda h: (h, 0, 0))],
        out_specs=pl.BlockSpec((1, Q_BLOCK, K_WINDOW), lambda h: (h, 0, 0)),
        out_shape=jax.ShapeDtypeStruct((heads, Q_BLOCK, K_WINDOW), jnp.float32),
        compiler_params=pltpu.CompilerParams(
            dimension_semantics=("parallel",),
            vmem_limit_bytes=_vmem_limit(
                _nbytes((Q_BLOCK, K_WINDOW), jnp.float32), 0,
                6 * _nbytes((Q_BLOCK, width), jnp.float32))),
        name="bias_table",
    )(g.reshape(heads, 1, width))


def _attn_block(q, k, v, bias):
    s = lax.dot_general(q, k, (((1,), (1,)), ((), ())),
                        preferred_element_type=jnp.float32)
    s = s * (A_HEAD_DIM ** -0.5) + bias
    m = jnp.max(s, axis=-1, keepdims=True)
    p = jnp.exp(s - m)
    p = p * (1.0 / jnp.sum(p, axis=-1, keepdims=True))
    return jnp.dot(p.astype(v.dtype), v, preferred_element_type=jnp.float32)


def _attention_kernel(q_ref, k_ref, v_ref, b_ref, o_ref):
    qb = pl.program_id(2)
    left = K_WINDOW - Q_BLOCK

    @pl.when(qb == 0)
    def _():
        o_ref[...] = _attn_block(q_ref[...], k_ref[:Q_BLOCK], v_ref[:Q_BLOCK],
                                 b_ref[0, :, left:]).astype(o_ref.dtype)

    @pl.when(qb > 0)
    def _():
        start = pl.multiple_of(qb * Q_BLOCK - left, CHUNK)
        o_ref[...] = _attn_block(q_ref[...], k_ref[pl.ds(start, K_WINDOW)],
                                 v_ref[pl.ds(start, K_WINDOW)],
                                 b_ref[0]).astype(o_ref.dtype)


def _attention(qkv, bias, *, batch, seq, heads):
    dh = A_HEAD_DIM
    n_qb = seq // Q_BLOCK
    assert seq % Q_BLOCK == 0 and K_WINDOW - Q_BLOCK <= Q_BLOCK
    pipelined = (2 * _nbytes((Q_BLOCK, dh), qkv.dtype) + 2 * _nbytes((seq, dh), qkv.dtype)
                 + _nbytes((Q_BLOCK, K_WINDOW), jnp.float32))
    temps = 6 * _nbytes((Q_BLOCK, K_WINDOW), jnp.float32)
    return pl.pallas_call(
        _attention_kernel,
        grid=(batch, heads, n_qb),
        in_specs=[
            pl.BlockSpec((Q_BLOCK, dh), lambda b, h, i: (b * n_qb + i, h)),
            pl.BlockSpec((seq, dh), lambda b, h, i: (b, heads + h)),
            pl.BlockSpec((seq, dh), lambda b, h, i: (b, 2 * heads + h)),
            pl.BlockSpec((1, Q_BLOCK, K_WINDOW), lambda b, h, i: (h, 0, 0)),
        ],
        out_specs=pl.BlockSpec((Q_BLOCK, dh), lambda b, h, i: (b * n_qb + i, h)),
        out_shape=jax.ShapeDtypeStruct((batch * seq, heads * dh), qkv.dtype),
        compiler_params=pltpu.CompilerParams(
            dimension_semantics=("parallel", "parallel", "arbitrary"),
            vmem_limit_bytes=_vmem_limit(pipelined, 0, temps)),
        name="band_attention",
    )(qkv, qkv, qkv, bias)


def _conv_out_proj_kernel(x_ref, a_ref, bg_ref, cg_ref, hv_ref, cprev_ref, hprev_ref,
                          cw_ref, w_ref, o_ref, z_ref, lhs_ref, *, rows_per_seq):
    tm, aw = a_ref.shape
    halo = V7X_SUBLANES
    i = pl.program_id(0)
    at_seq_start = (i * tm) % rows_per_seq == 0
    zprev = cprev_ref[...] * hprev_ref[...]
    z_ref[:halo] = jnp.where(at_seq_start, 0.0, zprev)
    z_ref[halo:] = cg_ref[...] * hv_ref[...]
    y = cw_ref[0:1] * z_ref[pl.ds(halo - 2, tm)]
    y = y + cw_ref[1:2] * z_ref[pl.ds(halo - 1, tm)]
    y = y + cw_ref[2:3] * z_ref[pl.ds(halo, tm)]
    lhs_ref[:, :aw] = a_ref[...]
    lhs_ref[:, aw:] = (bg_ref[...] * y).astype(lhs_ref.dtype)
    o_ref[...] = x_ref[...] + jnp.dot(lhs_ref[...], w_ref[...],
                                      preferred_element_type=jnp.float32)


def _conv_out_proj(x, attn, gates, conv_w, w_out, *, seq):
    m, d = x.shape
    aw = attn.shape[1]
    bw = gates.shape[1] // 3
    tm = ROW_TILE_MIX
    halo = V7X_SUBLANES
    assert m % tm == 0 and seq % tm == 0 and CONV_WIDTH - 1 <= halo
    assert conv_w.shape == (CONV_WIDTH, bw) and CONV_WIDTH == 3
    halo_blocks = tm // halo

    def prev_rows(col):
        return pl.BlockSpec(
            (halo, bw), lambda i: (jnp.maximum(i * halo_blocks - 1, 0), col))

    pipelined = (2 * _nbytes((tm, d), x.dtype) + _nbytes((tm, aw), attn.dtype)
                 + 3 * _nbytes((tm, bw), gates.dtype) + _nbytes(w_out.shape, w_out.dtype))
    scratch = _nbytes((tm + halo, bw), jnp.float32) + _nbytes((tm, aw + bw), jnp.bfloat16)
    temps = 4 * _nbytes((tm, bw), jnp.float32) + _nbytes((tm, d), jnp.float32)
    return pl.pallas_call(
        functools.partial(_conv_out_proj_kernel, rows_per_seq=seq),
        grid=(m // tm,),
        in_specs=[
            pl.BlockSpec((tm, d), lambda i: (i, 0)),
            pl.BlockSpec((tm, aw), lambda i: (i, 0)),
            pl.BlockSpec((tm, bw), lambda i: (i, 0)),
            pl.BlockSpec((tm, bw), lambda i: (i, 1)),
            pl.BlockSpec((tm, bw), lambda i: (i, 2)),
            prev_rows(1),
            prev_rows(2),
            pl.BlockSpec((CONV_WIDTH, bw), lambda i: (0, 0)),
            pl.BlockSpec(w_out.shape, lambda i: (0, 0)),
        ],
        out_specs=pl.BlockSpec((tm, d), lambda i: (i, 0)),
        out_shape=jax.ShapeDtypeStruct((m, d), x.dtype),
        scratch_shapes=[pltpu.VMEM((tm + halo, bw), jnp.float32),
                        pltpu.VMEM((tm, aw + bw), jnp.bfloat16)],
        compiler_params=pltpu.CompilerParams(
            dimension_semantics=("parallel",),
            vmem_limit_bytes=_vmem_limit(pipelined, scratch, temps)),
        name="conv_out_proj",
    )(x, attn, gates, gates, gates, gates, gates, conv_w, w_out)


def _sgu_out_proj_kernel(x_ref, u_ref, v_ref, lg_ref, lb_ref, ws_ref, bs_ref, w_ref,
                         o_ref, vn_ref, lhs_ref):
    tm, cw = u_ref.shape
    gw = cw // C_GROUPS
    v = v_ref[...]
    mu = jnp.mean(v, axis=-1, keepdims=True)
    vc = v - mu
    var = jnp.mean(vc * vc, axis=-1, keepdims=True)
    vn_ref[...] = (vc * lax.rsqrt(var + EPS) * lg_ref[...] + lb_ref[...]).astype(vn_ref.dtype)

    t_chunk = lax.broadcasted_iota(jnp.int32, (C_BLOCK, C_BLOCK), 0) // CHUNK
    s_chunk = lax.broadcasted_iota(jnp.int32, (C_BLOCK, C_BLOCK), 1) // CHUNK
    causal = s_chunk <= t_chunk
    for g in range(C_GROUPS):
        w_m = jnp.where(causal, ws_ref[g], 0.0).astype(vn_ref.dtype)
        bias = bs_ref[:, g:g + 1]
        cols = slice(g * gw, (g + 1) * gw)
        for n in range(tm // C_BLOCK):
            rows = slice(n * C_BLOCK, (n + 1) * C_BLOCK)
            s = jnp.dot(w_m, vn_ref[rows, cols], preferred_element_type=jnp.float32) + bias
            lhs_ref[rows, cols] = (u_ref[rows, cols] * s).astype(lhs_ref.dtype)
    o_ref[...] = x_ref[...] + jnp.dot(lhs_ref[...], w_ref[...],
                                      preferred_element_type=jnp.float32)


def _sgu_out_proj(x, z, ln_g, ln_b, w_s, b_s, w_out):
    m, d = x.shape
    cw = z.shape[1] // 2
    tm = ROW_TILE_MIX
    assert m % tm == 0 and tm % C_BLOCK == 0 and cw % C_GROUPS == 0
    pipelined = (2 * _nbytes((tm, d), x.dtype) + 2 * _nbytes((tm, cw), z.dtype)
                 + _nbytes(w_s.shape, w_s.dtype) + _nbytes(w_out.shape, w_out.dtype))
    scratch = 2 * _nbytes((tm, cw), jnp.bfloat16)
    temps = 3 * _nbytes((tm, cw), jnp.float32)
    return pl.pallas_call(
        _sgu_out_proj_kernel,
        grid=(m // tm,),
        in_specs=[
            pl.BlockSpec((tm, d), lambda i: (i, 0)),
            pl.BlockSpec((tm, cw), lambda i: (i, 0)),
            pl.BlockSpec((tm, cw), lambda i: (i, 1)),
            pl.BlockSpec((1, cw), lambda i: (0, 0)),
            pl.BlockSpec((1, cw), lambda i: (0, 0)),
            pl.BlockSpec(w_s.shape, lambda i: (0, 0, 0)),
            pl.BlockSpec((C_BLOCK, C_GROUPS), lambda i: (0, 0)),
            pl.BlockSpec(w_out.shape, lambda i: (0, 0)),
        ],
        out_specs=pl.BlockSpec((tm, d), lambda i: (i, 0)),
        out_shape=jax.ShapeDtypeStruct((m, d), x.dtype),
        scratch_shapes=[pltpu.VMEM((tm, cw), jnp.bfloat16),
                        pltpu.VMEM((tm, cw), jnp.bfloat16)],
        compiler_params=pltpu.CompilerParams(
            dimension_semantics=("parallel",),
            vmem_limit_bytes=_vmem_limit(pipelined, scratch, temps)),
        name="sgu_out_proj",
    )(x, z, z, ln_g.reshape(1, cw), ln_b.reshape(1, cw), w_s, jnp.transpose(b_s), w_out)


def _ffn_kernel(x_ref, g_ref, wg_ref, wu_ref, wd_ref, fg_ref, o_ref, h_ref, *, final_norm):
    j = pl.program_id(1)

    @pl.when(j == 0)
    def _():
        x = x_ref[...]
        h_ref[...] = _rms_norm(x, g_ref[...]).astype(h_ref.dtype)
        o_ref[...] = x

    h = h_ref[...]
    gate = jnp.dot(h, wg_ref[...], preferred_element_type=jnp.float32)
    up = jnp.dot(h, wu_ref[...], preferred_element_type=jnp.float32)
    act = (gate * jax.nn.sigmoid(gate) * up).astype(h.dtype)
    o_ref[...] += jnp.dot(act, wd_ref[...], preferred_element_type=jnp.float32)

    if final_norm:
        @pl.when(j == pl.num_programs(1) - 1)
        def _():
            o_ref[...] = _rms_norm(o_ref[...], fg_ref[...])


def _ffn(x, g, w_gate, w_up, w_down, final_g, *, final_norm):
    m, d = x.shape
    f = w_gate.shape[1]
    tm, tf = ROW_TILE_PROJ, COL_TILE_FFN
    assert m % tm == 0 and f % tf == 0
    pipelined = (_nbytes((tm, d), x.dtype) + 3 * _nbytes((d, tf), w_gate.dtype))
    scratch = _nbytes((tm, d), jnp.bfloat16) + _nbytes((tm, d), x.dtype)
    temps = 4 * _nbytes((tm, tf), jnp.float32) + _nbytes((tm, d), jnp.float32)
    return pl.pallas_call(
        functools.partial(_ffn_kernel, final_norm=final_norm),
        grid=(m // tm, f // tf),
        in_specs=[
            pl.BlockSpec((tm, d), lambda i, j: (i, 0), pipeline_mode=pl.Buffered(1)),
            pl.BlockSpec((1, d), lambda i, j: (0, 0)),
            pl.BlockSpec((d, tf), lambda i, j: (0, j)),
            pl.BlockSpec((d, tf), lambda i, j: (0, j)),
            pl.BlockSpec((tf, d), lambda i, j: (j, 0)),
            pl.BlockSpec((1, d), lambda i, j: (0, 0)),
        ],
        out_specs=pl.BlockSpec((tm, d), lambda i, j: (i, 0)),
        out_shape=jax.ShapeDtypeStruct((m, d), x.dtype),
        scratch_shapes=[pltpu.VMEM((tm, d), jnp.bfloat16)],
        compiler_params=pltpu.CompilerParams(
            dimension_semantics=("parallel", "arbitrary"),
            vmem_limit_bytes=_vmem_limit(pipelined, scratch, temps)),
        name="ffn_final" if final_norm else "ffn",
    )(x, g.reshape(1, d), w_gate, w_up, w_down, final_g.reshape(1, d))


def kernel(x, mix_norm, ab_w_in, ab_rel_bias, ab_conv_w, ab_w_out, c_w_in, c_ln_g, c_ln_b,
           c_w_s, c_b_s, c_w_out, ffn_norm, ffn_w_gate, ffn_w_up, ffn_w_down, final_norm):
    batch, seq, d = x.shape
    depth = mix_norm.shape[0]
    heads = ab_rel_bias.shape[1]
    a_width = heads * A_HEAD_DIM
    b_width = ab_conv_w.shape[2]
    bf16 = jnp.bfloat16

    xf = x.reshape(batch * seq, d)
    for layer in range(depth):
        i = layer // 2
        if layer % 2 == 0:
            w_in = ab_w_in[i].astype(bf16)
            qkv = _norm_proj(xf, mix_norm[layer], w_in, col_start=0, n_cols=3 * a_width,
                             out_dtype=bf16, gelu=False, name="norm_proj_qkv")
            gates = _norm_proj(xf, mix_norm[layer], w_in, col_start=3 * a_width,
                               n_cols=3 * b_width, out_dtype=jnp.float32, gelu=False,
                               name="norm_proj_gates")
            attn = _attention(qkv, _bias_table(ab_rel_bias[i]), batch=batch, seq=seq,
                              heads=heads)
            xf = _conv_out_proj(xf, attn, gates, ab_conv_w[i], ab_w_out[i].astype(bf16),
                                seq=seq)
        else:
            w_in = c_w_in[i].astype(bf16)
            z = _norm_proj(xf, mix_norm[layer], w_in, col_start=0, n_cols=w_in.shape[1],
                           out_dtype=jnp.float32, gelu=True, name="norm_proj_gelu")
            xf = _sgu_out_proj(xf, z, c_ln_g[i], c_ln_b[i], c_w_s[i], c_b_s[i],
                               c_w_out[i].astype(bf16))
        xf = _ffn(xf, ffn_norm[layer], ffn_w_gate[layer].astype(bf16),
                  ffn_w_up[layer].astype(bf16), ffn_w_down[layer].astype(bf16),
                  final_norm, final_norm=(layer == depth - 1))
    return xf.reshape(batch, seq, d)
```

```python
import functools

import jax
import jax.numpy as jnp
import numpy as np
from jax import lax
from jax.experimental import pallas as pl
from jax.experimental.pallas import tpu as pltpu

CHUNK = 64
A_HEAD_DIM = 128
A_LEFT_CHUNKS = 8
A_MAX_REL = 256
CONV_WIDTH = 3
C_BLOCK = 128
C_GROUPS = 8
EPS = 1e-6
NEG_INF = -1e30

V7X_VMEM_BYTES = 64 * 1024 * 1024
V7X_SUBLANES = 8

ROW_TILE_PROJ = 1024
ROW_TILE_MIX = 512
COL_TILE_PROJ = 1024
COL_TILE_FFN = 512
Q_BLOCK = 2 * CHUNK
K_WINDOW = Q_BLOCK + A_LEFT_CHUNKS * CHUNK

_SQRT_HALF = np.sqrt(0.5).astype(np.float32)


def _vmem_limit(pipelined_bytes, scratch_bytes, temp_bytes):
    need = 2 * pipelined_bytes + scratch_bytes + temp_bytes
    return int(min(need + need // 8, V7X_VMEM_BYTES - 4 * 1024 * 1024))


def _nbytes(shape, dtype):
    return int(np.prod(shape)) * jnp.dtype(dtype).itemsize


def _rms_norm(x, g):
    ms = jnp.mean(x * x, axis=-1, keepdims=True)
    return (x * lax.rsqrt(ms + EPS)) * g


def _norm_proj_kernel(x_ref, g_ref, w_ref, o_ref, h_ref, *, gelu):
    @pl.when(pl.program_id(1) == 0)
    def _():
        h_ref[...] = _rms_norm(x_ref[...], g_ref[...]).astype(h_ref.dtype)

    y = jnp.dot(h_ref[...], w_ref[...], preferred_element_type=jnp.float32)
    if gelu:
        y = 0.5 * y * (1.0 + lax.erf(y * _SQRT_HALF))
    o_ref[...] = y.astype(o_ref.dtype)


def _norm_proj(x, g, w, *, col_start, n_cols, out_dtype, gelu, name):
    m, d = x.shape
    tm, tn = ROW_TILE_PROJ, COL_TILE_PROJ
    assert m % tm == 0 and n_cols % tn == 0 and col_start % tn == 0
    col0 = col_start // tn
    pipelined = (_nbytes((tm, d), x.dtype) + _nbytes((d, tn), w.dtype)
                 + _nbytes((tm, tn), out_dtype))
    scratch = _nbytes((tm, d), jnp.bfloat16)
    temps = 3 * _nbytes((tm, tn), jnp.float32)
    return pl.pallas_call(
        functools.partial(_norm_proj_kernel, gelu=gelu),
        grid=(m // tm, n_cols // tn),
        in_specs=[
            pl.BlockSpec((tm, d), lambda i, j: (i, 0)),
            pl.BlockSpec((1, d), lambda i, j: (0, 0)),
            pl.BlockSpec((d, tn), lambda i, j: (0, col0 + j)),
        ],
        out_specs=pl.BlockSpec((tm, tn), lambda i, j: (i, j)),
        out_shape=jax.ShapeDtypeStruct((m, n_cols), out_dtype),
        scratch_shapes=[pltpu.VMEM((tm, d), jnp.bfloat16)],
        compiler_params=pltpu.CompilerParams(
            dimension_semantics=("parallel", "arbitrary"),
            vmem_limit_bytes=_vmem_limit(pipelined, scratch, temps)),
        name=name,
    )(x, g.reshape(1, d), w)


def _bias_table_kernel(g_ref, o_ref):
    width = Q_BLOCK + K_WINDOW
    x = jnp.broadcast_to(g_ref[0], (Q_BLOCK, width))
    row = lax.broadcasted_iota(jnp.int32, (Q_BLOCK, width), 0)
    shift = 1
    while shift < Q_BLOCK:
        x = jnp.where((row & shift) != 0, pltpu.roll(x, shift, 1), x)
        shift *= 2
    t = x[:, Q_BLOCK:]
    q_chunk = lax.broadcasted_iota(jnp.int32, (Q_BLOCK, K_WINDOW), 0) // CHUNK
    k_chunk = lax.broadcasted_iota(jnp.int32, (Q_BLOCK, K_WINDOW), 1) // CHUNK
    band = k_chunk - q_chunk
    o_ref[0] = jnp.where((band >= 0) & (band <= A_LEFT_CHUNKS), t, NEG_INF)


def _bias_table(rel_bias):
    heads, n_rel = rel_bias.shape
    width = Q_BLOCK + K_WINDOW
    left = Q_BLOCK + A_LEFT_CHUNKS * CHUNK - A_MAX_REL
    right = max(0, width - left - n_rel)
    g = jnp.pad(rel_bias[:, ::-1], ((0, 0), (left, right)), mode="edge")[:, :width]
    return pl.pallas_call(
        _bias_table_kernel,
        grid=(heads,),
        in_specs=[pl.BlockSpec((1, 1, width), lambda h: (h, 0, 0))],
        out_specs=pl.BlockSpec((1, Q_BLOCK, K_WINDOW), lambda h: (h, 0, 0)),
        out_shape=jax.ShapeDtypeStruct((heads, Q_BLOCK, K_WINDOW), jnp.float32),
        compiler_params=pltpu.CompilerParams(
            dimension_semantics=("parallel",),
            vmem_limit_bytes=_vmem_limit(
                _nbytes((Q_BLOCK, K_WINDOW), jnp.float32), 0,
                6 * _nbytes((Q_BLOCK, width), jnp.float32))),
        name="bias_table",
    )(g.reshape(heads, 1, width))


def _attention_kernel(q_ref, k_ref, v_ref, b_ref, o_ref, kt_ref):
    seq = q_ref.shape[0]
    kt_ref[...] = k_ref[...].T
    for q0 in range(0, seq, Q_BLOCK):
        k0 = max(0, q0 + Q_BLOCK - K_WINDOW)
        k1 = q0 + Q_BLOCK
        s = jnp.dot(q_ref[q0:k1], kt_ref[:, k0:k1], preferred_element_type=jnp.float32)
        s = s * (A_HEAD_DIM ** -0.5) + b_ref[0, :, K_WINDOW - (k1 - k0):]
        p = jnp.exp(s - jnp.max(s, axis=-1, keepdims=True))
        inv_l = 1.0 / jnp.sum(p, axis=-1, keepdims=True)
        o = jnp.dot(p.astype(v_ref.dtype), v_ref[k0:k1], preferred_element_type=jnp.float32)
        o_ref[q0:k1] = (o * inv_l).astype(o_ref.dtype)


def _attention(qkv, bias, *, batch, seq, heads):
    dh = A_HEAD_DIM
    assert seq % Q_BLOCK == 0 and K_WINDOW % Q_BLOCK == 0
    pipelined = 4 * _nbytes((seq, dh), qkv.dtype) + _nbytes((Q_BLOCK, K_WINDOW), jnp.float32)
    scratch = _nbytes((dh, seq), qkv.dtype)
    temps = 16 * _nbytes((Q_BLOCK, K_WINDOW), jnp.float32)
    return pl.pallas_call(
        _attention_kernel,
        grid=(batch, heads),
        in_specs=[
            pl.BlockSpec((seq, dh), lambda b, h: (b, h)),
            pl.BlockSpec((seq, dh), lambda b, h: (b, heads + h)),
            pl.BlockSpec((seq, dh), lambda b, h: (b, 2 * heads + h)),
            pl.BlockSpec((1, Q_BLOCK, K_WINDOW), lambda b, h: (h, 0, 0)),
        ],
        out_specs=pl.BlockSpec((seq, dh), lambda b, h: (b, h)),
        out_shape=jax.ShapeDtypeStruct((batch * seq, heads * dh), qkv.dtype),
        scratch_shapes=[pltpu.VMEM((dh, seq), qkv.dtype)],
        compiler_params=pltpu.CompilerParams(
            dimension_semantics=("parallel", "parallel"),
            vmem_limit_bytes=_vmem_limit(pipelined, scratch, temps)),
        name="band_attention",
    )(qkv, qkv, qkv, bias)


def _conv_out_proj_kernel(x_ref, a_ref, bg_ref, cg_ref, hv_ref, cprev_ref, hprev_ref,
                          cw_ref, w_ref, o_ref, z_ref, lhs_ref, *, rows_per_seq):
    tm, aw = a_ref.shape
    halo = V7X_SUBLANES
    i = pl.program_id(0)
    at_seq_start = (i * tm) % rows_per_seq == 0
    zprev = cprev_ref[...] * hprev_ref[...]
    z_ref[:halo] = jnp.where(at_seq_start, 0.0, zprev)
    z_ref[halo:] = cg_ref[...] * hv_ref[...]
    y = cw_ref[0:1] * z_ref[pl.ds(halo - 2, tm)]
    y = y + cw_ref[1:2] * z_ref[pl.ds(halo - 1, tm)]
    y = y + cw_ref[2:3] * z_ref[pl.ds(halo, tm)]
    lhs_ref[:, :aw] = a_ref[...]
    lhs_ref[:, aw:] = (bg_ref[...] * y).astype(lhs_ref.dtype)
    o_ref[...] = x_ref[...] + jnp.dot(lhs_ref[...], w_ref[...],
                                      preferred_element_type=jnp.float32)


def _conv_out_proj(x, attn, gates, conv_w, w_out, *, seq):
    m, d = x.shape
    aw = attn.shape[1]
    bw = gates.shape[1] // 3
    tm = ROW_TILE_MIX
    halo = V7X_SUBLANES
    assert m % tm == 0 and seq % tm == 0 and CONV_WIDTH - 1 <= halo
    assert conv_w.shape == (CONV_WIDTH, bw) and CONV_WIDTH == 3
    halo_blocks = tm // halo

    def prev_rows(col):
        return pl.BlockSpec(
            (halo, bw), lambda i: (jnp.maximum(i * halo_blocks - 1, 0), col))

    pipelined = (2 * _nbytes((tm, d), x.dtype) + _nbytes((tm, aw), attn.dtype)
                 + 3 * _nbytes((tm, bw), gates.dtype) + _nbytes(w_out.shape, w_out.dtype))
    scratch = _nbytes((tm + halo, bw), jnp.float32) + _nbytes((tm, aw + bw), jnp.bfloat16)
    temps = 4 * _nbytes((tm, bw), jnp.float32) + _nbytes((tm, d), jnp.float32)
    return pl.pallas_call(
        functools.partial(_conv_out_proj_kernel, rows_per_seq=seq),
        grid=(m // tm,),
        in_specs=[
            pl.BlockSpec((tm, d), lambda i: (i, 0)),
            pl.BlockSpec((tm, aw), lambda i: (i, 0)),
            pl.BlockSpec((tm, bw), lambda i: (i, 0)),
            pl.BlockSpec((tm, bw), lambda i: (i, 1)),
            pl.BlockSpec((tm, bw), lambda i: (i, 2)),
            prev_rows(1),
            prev_rows(2),
            pl.BlockSpec((CONV_WIDTH, bw), lambda i: (0, 0)),
            pl.BlockSpec(w_out.shape, lambda i: (0, 0)),
        ],
        out_specs=pl.BlockSpec((tm, d), lambda i: (i, 0)),
        out_shape=jax.ShapeDtypeStruct((m, d), x.dtype),
        scratch_shapes=[pltpu.VMEM((tm + halo, bw), jnp.float32),
                        pltpu.VMEM((tm, aw + bw), jnp.bfloat16)],
        compiler_params=pltpu.CompilerParams(
            dimension_semantics=("parallel",),
            vmem_limit_bytes=_vmem_limit(pipelined, scratch, temps)),
        name="conv_out_proj",
    )(x, attn, gates, gates, gates, gates, gates, conv_w, w_out)


def _sgu_out_proj_kernel(x_ref, u_ref, v_ref, lg_ref, lb_ref, ws_ref, bs_ref, w_ref,
                         o_ref, vn_ref, lhs_ref):
    tm, cw = u_ref.shape
    gw = cw // C_GROUPS
    v = v_ref[...]
    mu = jnp.mean(v, axis=-1, keepdims=True)
    vc = v - mu
    var = jnp.mean(vc * vc, axis=-1, keepdims=True)
    vn_ref[...] = (vc * lax.rsqrt(var + EPS) * lg_ref[...] + lb_ref[...]).astype(vn_ref.dtype)

    t_chunk = lax.broadcasted_iota(jnp.int32, (C_BLOCK, C_BLOCK), 0) // CHUNK
    s_chunk = lax.broadcasted_iota(jnp.int32, (C_BLOCK, C_BLOCK), 1) // CHUNK
    causal = s_chunk <= t_chunk
    for g in range(C_GROUPS):
        w_m = jnp.where(causal, ws_ref[g], 0.0).astype(vn_ref.dtype)
        bias = bs_ref[:, g:g + 1]
        cols = slice(g * gw, (g + 1) * gw)
        for n in range(tm // C_BLOCK):
            rows = slice(n * C_BLOCK, (n + 1) * C_BLOCK)
            s = jnp.dot(w_m, vn_ref[rows, cols], preferred_element_type=jnp.float32) + bias
            lhs_ref[rows, cols] = (u_ref[rows, cols] * s).astype(lhs_ref.dtype)
    o_ref[...] = x_ref[...] + jnp.dot(lhs_ref[...], w_ref[...],
                                      preferred_element_type=jnp.float32)


def _sgu_out_proj(x, z, ln_g, ln_b, w_s, b_s, w_out):
    m, d = x.shape
    cw = z.shape[1] // 2
    tm = ROW_TILE_MIX
    assert m % tm == 0 and tm % C_BLOCK == 0 and cw % C_GROUPS == 0
    pipelined = (2 * _nbytes((tm, d), x.dtype) + 2 * _nbytes((tm, cw), z.dtype)
                 + _nbytes(w_s.shape, w_s.dtype) + _nbytes(w_out.shape, w_out.dtype))
    scratch = 2 * _nbytes((tm, cw), jnp.bfloat16)
    temps = 3 * _nbytes((tm, cw), jnp.float32)
    return pl.pallas_call(
        _sgu_out_proj_kernel,
        grid=(m // tm,),
        in_specs=[
            pl.BlockSpec((tm, d), lambda i: (i, 0)),
            pl.BlockSpec((tm, cw), lambda i: (i, 0)),
            pl.BlockSpec((tm, cw), lambda i: (i, 1)),
            pl.BlockSpec((1, cw), lambda i: (0, 0)),
            pl.BlockSpec((1, cw), lambda i: (0, 0)),
            pl.BlockSpec(w_s.shape, lambda i: (0, 0, 0)),
            pl.BlockSpec((C_BLOCK, C_GROUPS), lambda i: (0, 0)),
            pl.BlockSpec(w_out.shape, lambda i: (0, 0)),
        ],
        out_specs=pl.BlockSpec((tm, d), lambda i: (i, 0)),
        out_shape=jax.ShapeDtypeStruct((m, d), x.dtype),
        scratch_shapes=[pltpu.VMEM((tm, cw), jnp.bfloat16),
                        pltpu.VMEM((tm, cw), jnp.bfloat16)],
        compiler_params=pltpu.CompilerParams(
            dimension_semantics=("parallel",),
            vmem_limit_bytes=_vmem_limit(pipelined, scratch, temps)),
        name="sgu_out_proj",
    )(x, z, z, ln_g.reshape(1, cw), ln_b.reshape(1, cw), w_s, jnp.transpose(b_s), w_out)


def _cast_specs(stacked, index, gi, gj):
    _, r, c = stacked.shape
    if r % (gi * 16) == 0:
        nj = max(n for n in range(1, gj + 1) if c % (n * 128) == 0)
        block = (r // gi, c // nj)

        def imap(i, j):
            return (i, jnp.minimum(j, nj - 1))
    else:
        assert r % (gj * 16) == 0 and c % (gi * 128) == 0
        block = (r // gj, c // gi)

        def imap(i, j):
            return (j, i)

    in_spec = pl.BlockSpec((None,) + block, lambda i, j: (index,) + imap(i, j))
    out_spec = pl.BlockSpec(block, imap)
    return in_spec, out_spec, jax.ShapeDtypeStruct((r, c), jnp.bfloat16), block


def _ffn_kernel(*refs, final_norm, n_cast):
    x_ref, g_ref, wg_ref, wu_ref, wd_ref, fg_ref = refs[:6]
    cast_in = refs[6:6 + n_cast]
    o_ref = refs[6 + n_cast]
    cast_out = refs[7 + n_cast:7 + 2 * n_cast]
    h_ref = refs[7 + 2 * n_cast]
    j = pl.program_id(1)

    for src, dst in zip(cast_in, cast_out):
        dst[...] = src[...].astype(dst.dtype)

    @pl.when(j == 0)
    def _():
        x = x_ref[...]
        h_ref[...] = _rms_norm(x, g_ref[...]).astype(h_ref.dtype)
        o_ref[...] = x

    h = h_ref[...]
    gate = jnp.dot(h, wg_ref[...], preferred_element_type=jnp.float32)
    up = jnp.dot(h, wu_ref[...], preferred_element_type=jnp.float32)
    act = (gate * jax.nn.sigmoid(gate) * up).astype(h.dtype)
    o_ref[...] += jnp.dot(act, wd_ref[...], preferred_element_type=jnp.float32)

    if final_norm:
        @pl.when(j == pl.num_programs(1) - 1)
        def _():
            o_ref[...] = _rms_norm(o_ref[...], fg_ref[...])


def _ffn(x, g, w_gate, w_up, w_down, final_g, *, final_norm, casts):
    m, d = x.shape
    f = w_gate.shape[1]
    tm, tf = ROW_TILE_PROJ, COL_TILE_FFN
    assert m % tm == 0 and f % tf == 0
    grid = (m // tm, f // tf)
    cast_specs = [_cast_specs(arr, idx, *grid) for arr, idx in casts]
    cast_bytes = sum(_nbytes(blk, jnp.float32) + _nbytes(blk, jnp.bfloat16)
                     for _, _, _, blk in cast_specs)
    out_bytes = _nbytes((tm, d), x.dtype)
    pipelined = out_bytes + 3 * _nbytes((d, tf), w_gate.dtype) + cast_bytes
    scratch = _nbytes((tm, d), jnp.bfloat16) + _nbytes((tm, d), x.dtype)
    temps = 4 * _nbytes((tm, tf), jnp.float32)
    outs = pl.pallas_call(
        functools.partial(_ffn_kernel, final_norm=final_norm, n_cast=len(casts)),
        grid=grid,
        in_specs=[
            pl.BlockSpec((tm, d), lambda i, j: (i, 0), pipeline_mode=pl.Buffered(1)),
            pl.BlockSpec((1, d), lambda i, j: (0, 0)),
            pl.BlockSpec((d, tf), lambda i, j: (0, j)),
            pl.BlockSpec((d, tf), lambda i, j: (0, j)),
            pl.BlockSpec((tf, d), lambda i, j: (j, 0)),
            pl.BlockSpec((1, d), lambda i, j: (0, 0)),
        ] + [s[0] for s in cast_specs],
        out_specs=[pl.BlockSpec((tm, d), lambda i, j: (i, 0))] + [s[1] for s in cast_specs],
        out_shape=[jax.ShapeDtypeStruct((m, d), x.dtype)] + [s[2] for s in cast_specs],
        scratch_shapes=[pltpu.VMEM((tm, d), jnp.bfloat16)],
        compiler_params=pltpu.CompilerParams(
            dimension_semantics=("arbitrary", "arbitrary"),
            vmem_limit_bytes=_vmem_limit(pipelined, scratch, temps)),
        name="ffn_final" if final_norm else "ffn",
    )(x, g.reshape(1, d), w_gate, w_up, w_down, final_g.reshape(1, d),
      *[arr for arr, _ in casts])
    return outs[0], list(outs[1:])


def kernel(x, mix_norm, ab_w_in, ab_rel_bias, ab_conv_w, ab_w_out, c_w_in, c_ln_g, c_ln_b,
           c_w_s, c_b_s, c_w_out, ffn_norm, ffn_w_gate, ffn_w_up, ffn_w_down, final_norm):
    batch, seq, d = x.shape
    depth = mix_norm.shape[0]
    heads = ab_rel_bias.shape[1]
    a_width = heads * A_HEAD_DIM
    b_width = ab_conv_w.shape[2]
    bf16 = jnp.bfloat16

    xf = x.reshape(batch * seq, d)
    w_in, w_out = ab_w_in[0].astype(bf16), ab_w_out[0].astype(bf16)
    w_ffn = [w[0].astype(bf16) for w in (ffn_w_gate, ffn_w_up, ffn_w_down)]
    for layer in range(depth):
        i = layer // 2
        if layer % 2 == 0:
            qkv = _norm_proj(xf, mix_norm[layer], w_in, col_start=0, n_cols=3 * a_width,
                             out_dtype=bf16, gelu=False, name="norm_proj_qkv")
            gates = _norm_proj(xf, mix_norm[layer], w_in, col_start=3 * a_width,
                               n_cols=3 * b_width, out_dtype=jnp.float32, gelu=False,
                               name="norm_proj_gates")
            attn = _attention(qkv, _bias_table(ab_rel_bias[i]), batch=batch, seq=seq,
                              heads=heads)
            xf = _conv_out_proj(xf, attn, gates, ab_conv_w[i], w_out, seq=seq)
        else:
            z = _norm_proj(xf, mix_norm[layer], w_in, col_start=0, n_cols=w_in.shape[1],
                           out_dtype=jnp.float32, gelu=True, name="norm_proj_gelu")
            xf = _sgu_out_proj(xf, z, c_ln_g[i], c_ln_b[i], c_w_s[i], c_b_s[i], w_out)
        nxt = layer + 1
        casts = []
        if nxt < depth:
            mixer = (c_w_in, c_w_out) if nxt % 2 else (ab_w_in, ab_w_out)
            casts = ([(w, nxt) for w in (ffn_w_gate, ffn_w_up, ffn_w_down)]
                     + [(w, nxt // 2) for w in mixer])
        xf, w_next = _ffn(xf, ffn_norm[layer], *w_ffn, final_norm,
                          final_norm=(nxt == depth), casts=casts)
        if w_next:
            w_ffn, (w_in, w_out) = w_next[:3], w_next[3:]
    return xf.reshape(batch, seq, d)
```

```python
import functools

import jax
import jax.numpy as jnp
import numpy as np
from jax import lax
from jax.experimental import pallas as pl
from jax.experimental.pallas import tpu as pltpu

CHUNK = 64
A_HEAD_DIM = 128
A_LEFT_CHUNKS = 8
A_MAX_REL = 256
CONV_WIDTH = 3
C_BLOCK = 128
C_GROUPS = 8
EPS = 1e-6
NEG_INF = -1e30

V7X_VMEM_BYTES = 64 * 1024 * 1024
V7X_LANES = 128
V7X_BF16_SUBLANES = 16

ROW_TILE_PROJ = 1024
ROW_TILE_MIX = 512
COL_TILE_PROJ = 1024
COL_TILE_FFN = 512
Q_BLOCK = 2 * CHUNK
K_WINDOW = Q_BLOCK + A_LEFT_CHUNKS * CHUNK
ATTN_INTERLEAVE = 8

_SQRT_HALF = np.sqrt(0.5).astype(np.float32)


def _vmem_limit(pipelined_bytes, scratch_bytes, temp_bytes):
    need = 2 * pipelined_bytes + scratch_bytes + temp_bytes
    return int(min(need + need // 8, V7X_VMEM_BYTES - 4 * 1024 * 1024))


def _nbytes(shape, dtype):
    return int(np.prod(shape)) * jnp.dtype(dtype).itemsize


def _rms_norm(x, g):
    ms = jnp.mean(x * x, axis=-1, keepdims=True)
    return (x * lax.rsqrt(ms + EPS)) * g


def _cast_plan(casts, grid):
    n_steps = int(np.prod(grid))
    strides = [int(np.prod(grid[a + 1:])) for a in range(len(grid))]
    in_specs, out_specs, out_shapes, nbytes = [], [], [], 0
    for stacked, index in casts:
        _, r, c = stacked.shape
        n_blocks = max(n for n in range(1, n_steps + 1) if r % (n * V7X_BF16_SUBLANES) == 0)
        rows = r // n_blocks

        def block_of(*ids, n_blocks=n_blocks):
            step = sum(i * s for i, s in zip(ids, strides))
            return jnp.minimum(step, n_blocks - 1)

        in_specs.append(pl.BlockSpec(
            (None, rows, c),
            lambda *ids, index=index, block_of=block_of: (index, block_of(*ids), 0)))
        out_specs.append(pl.BlockSpec(
            (rows, c), lambda *ids, block_of=block_of: (block_of(*ids), 0)))
        out_shapes.append(jax.ShapeDtypeStruct((r, c), jnp.bfloat16))
        nbytes += _nbytes((rows, c), jnp.float32) + _nbytes((rows, c), jnp.bfloat16)
    return in_specs, out_specs, out_shapes, nbytes


def _hosting_casts(body, n_in, n_out, n_cast):
    def kernel_fn(*refs):
        ins, rest = refs[:n_in], refs[n_in:]
        cast_in, rest = rest[:n_cast], rest[n_cast:]
        outs, rest = rest[:n_out], rest[n_out:]
        cast_out, scratch = rest[:n_cast], rest[n_cast:]
        body(*ins, *outs, *scratch, casts=tuple(zip(cast_in, cast_out)))
    return kernel_fn


def _run_casts(casts):
    for src, dst in casts:
        dst[...] = src[...].astype(dst.dtype)


def _norm_proj_kernel(x_ref, g_ref, w_ref, o_ref, h_ref, *, gelu, casts):
    @pl.when(pl.program_id(1) == 0)
    def _():
        h_ref[...] = _rms_norm(x_ref[...], g_ref[...]).astype(h_ref.dtype)

    _run_casts(casts)
    y = jnp.dot(h_ref[...], w_ref[...], preferred_element_type=jnp.float32)
    if gelu:
        y = 0.5 * y * (1.0 + lax.erf(y * _SQRT_HALF))
    o_ref[...] = y.astype(o_ref.dtype)


def _norm_proj(x, g, w, *, gelu, name, casts=()):
    m, d = x.shape
    n = w.shape[1]
    tm, tn = ROW_TILE_PROJ, COL_TILE_PROJ
    assert m % tm == 0 and n % tn == 0
    grid = (m // tm, n // tn)
    cast_in, cast_out, cast_shapes, cast_bytes = _cast_plan(casts, grid)
    out_dtype = jnp.bfloat16
    pipelined = (_nbytes((tm, d), x.dtype) + _nbytes((d, tn), w.dtype)
                 + _nbytes((tm, tn), out_dtype) + cast_bytes)
    scratch = _nbytes((tm, d), jnp.bfloat16)
    temps = 2 * _nbytes((tm, tn), jnp.float32)
    outs = pl.pallas_call(
        _hosting_casts(functools.partial(_norm_proj_kernel, gelu=gelu), 3, 1, len(casts)),
        grid=grid,
        in_specs=[
            pl.BlockSpec((tm, d), lambda i, j: (i, 0)),
            pl.BlockSpec((1, d), lambda i, j: (0, 0)),
            pl.BlockSpec((d, tn), lambda i, j: (0, j)),
        ] + cast_in,
        out_specs=[pl.BlockSpec((tm, tn), lambda i, j: (i, j))] + cast_out,
        out_shape=[jax.ShapeDtypeStruct((m, n), out_dtype)] + cast_shapes,
        scratch_shapes=[pltpu.VMEM((tm, d), jnp.bfloat16)],
        compiler_params=pltpu.CompilerParams(
            dimension_semantics=("arbitrary", "arbitrary"),
            vmem_limit_bytes=_vmem_limit(pipelined, scratch, temps)),
        name=name,
    )(x, g.reshape(1, d), w, *[arr for arr, _ in casts])
    return outs[0], list(outs[1:])


def _bias_table_kernel(g_ref, o_ref):
    width = Q_BLOCK + K_WINDOW
    x = jnp.broadcast_to(g_ref[0], (Q_BLOCK, width))
    row = lax.broadcasted_iota(jnp.int32, (Q_BLOCK, width), 0)
    shift = 1
    while shift < Q_BLOCK:
        x = jnp.where((row & shift) != 0, pltpu.roll(x, shift, 1), x)
        shift *= 2
    t = x[:, Q_BLOCK:]
    q_chunk = lax.broadcasted_iota(jnp.int32, (Q_BLOCK, K_WINDOW), 0) // CHUNK
    k_chunk = lax.broadcasted_iota(jnp.int32, (Q_BLOCK, K_WINDOW), 1) // CHUNK
    band = k_chunk - q_chunk
    o_ref[0] = jnp.where((band >= 0) & (band <= A_LEFT_CHUNKS), t, NEG_INF)


def _bias_table(rel_bias):
    heads, n_rel = rel_bias.shape
    width = Q_BLOCK + K_WINDOW
    left = Q_BLOCK + A_LEFT_CHUNKS * CHUNK - A_MAX_REL
    right = max(0, width - left - n_rel)
    g = jnp.pad(rel_bias[:, ::-1], ((0, 0), (left, right)), mode="edge")[:, :width]
    return pl.pallas_call(
        _bias_table_kernel,
        grid=(heads,),
        in_specs=[pl.BlockSpec((1, 1, width), lambda h: (h, 0, 0))],
        out_specs=pl.BlockSpec((1, Q_BLOCK, K_WINDOW), lambda h: (h, 0, 0)),
        out_shape=jax.ShapeDtypeStruct((heads, Q_BLOCK, K_WINDOW), jnp.float32),
        compiler_params=pltpu.CompilerParams(
            dimension_semantics=("parallel",),
            vmem_limit_bytes=_vmem_limit(
                _nbytes((Q_BLOCK, K_WINDOW), jnp.float32), 0,
                6 * _nbytes((Q_BLOCK, width), jnp.float32))),
        name="bias_table",
    )(g.reshape(heads, 1, width))


def _attention_kernel(q_ref, k_ref, v_ref, b_ref, o_ref, kt_ref, v1_ref, *, casts):
    seq, dh = q_ref.shape
    _run_casts(casts)
    kt_ref[...] = k_ref[...].T
    v1_ref[:, :dh] = v_ref[...]
    v1_ref[:, dh:] = jnp.ones((seq, dh), v1_ref.dtype)
    n_blocks = seq // Q_BLOCK
    n_groups = n_blocks // ATTN_INTERLEAVE
    for r in range(n_groups):
        win = []
        for blk in range(r, n_blocks, n_groups):
            q0 = blk * Q_BLOCK
            win.append((q0, max(0, q0 + Q_BLOCK - K_WINDOW), q0 + Q_BLOCK))
        s = [jnp.dot(q_ref[q0:k1], kt_ref[:, k0:k1], preferred_element_type=jnp.float32)
             for q0, k0, k1 in win]
        s = [si * (A_HEAD_DIM ** -0.5) + b_ref[0, :, K_WINDOW - (k1 - k0):]
             for si, (q0, k0, k1) in zip(s, win)]
        m = [jnp.max(si, axis=-1, keepdims=True) for si in s]
        p = [jnp.exp(si - mi).astype(v1_ref.dtype) for si, mi in zip(s, m)]
        o = [jnp.dot(pi, v1_ref[k0:k1], preferred_element_type=jnp.float32)
             for pi, (q0, k0, k1) in zip(p, win)]
        for oi, (q0, k0, k1) in zip(o, win):
            o_ref[q0:k1] = (oi[:, :dh] * (1.0 / oi[:, dh:dh + 1])).astype(o_ref.dtype)


def _attention(proj, bias, *, batch, seq, heads, casts=()):
    dh = A_HEAD_DIM
    assert seq % Q_BLOCK == 0 and K_WINDOW % Q_BLOCK == 0
    assert (seq // Q_BLOCK) % ATTN_INTERLEAVE == 0
    grid = (batch, heads)
    cast_in, cast_out, cast_shapes, cast_bytes = _cast_plan(casts, grid)
    pipelined = (4 * _nbytes((seq, dh), proj.dtype)
                 + _nbytes((Q_BLOCK, K_WINDOW), jnp.float32) + cast_bytes)
    scratch = 3 * _nbytes((seq, dh), proj.dtype)
    temps = 2 * ATTN_INTERLEAVE * _nbytes((Q_BLOCK, K_WINDOW), jnp.float32)
    outs = pl.pallas_call(
        _hosting_casts(_attention_kernel, 4, 1, len(casts)),
        grid=grid,
        in_specs=[
            pl.BlockSpec((seq, dh), lambda b, h: (b, h)),
            pl.BlockSpec((seq, dh), lambda b, h: (b, heads + h)),
            pl.BlockSpec((seq, dh), lambda b, h: (b, 2 * heads + h)),
            pl.BlockSpec((1, Q_BLOCK, K_WINDOW), lambda b, h: (h, 0, 0)),
        ] + cast_in,
        out_specs=[pl.BlockSpec((seq, dh), lambda b, h: (b, h))] + cast_out,
        out_shape=[jax.ShapeDtypeStruct((batch * seq, heads * dh), proj.dtype)] + cast_shapes,
        scratch_shapes=[pltpu.VMEM((dh, seq), proj.dtype),
                        pltpu.VMEM((seq, 2 * dh), proj.dtype)],
        compiler_params=pltpu.CompilerParams(
            dimension_semantics=("arbitrary", "arbitrary"),
            vmem_limit_bytes=_vmem_limit(pipelined, scratch, temps)),
        name="band_attention",
    )(proj, proj, proj, bias, *[arr for arr, _ in casts])
    return outs[0], list(outs[1:])


def _conv_out_proj_kernel(x_ref, a_ref, bg_ref, cg_ref, hv_ref, cprev_ref, hprev_ref,
                          cw_ref, w_ref, o_ref, z_ref, lhs_ref, *, rows_per_seq):
    tm, aw = a_ref.shape
    halo = cprev_ref.shape[0]
    f32 = jnp.float32
    i = pl.program_id(0)
    at_seq_start = (i * tm) % rows_per_seq == 0
    zprev = cprev_ref[...].astype(f32) * hprev_ref[...].astype(f32)
    z_ref[:halo] = jnp.where(at_seq_start, 0.0, zprev)
    z_ref[halo:] = cg_ref[...].astype(f32) * hv_ref[...].astype(f32)
    y = cw_ref[0:1] * z_ref[pl.ds(halo - 2, tm)]
    y = y + cw_ref[1:2] * z_ref[pl.ds(halo - 1, tm)]
    y = y + cw_ref[2:3] * z_ref[pl.ds(halo, tm)]
    lhs_ref[:, :aw] = a_ref[...]
    lhs_ref[:, aw:] = (bg_ref[...].astype(f32) * y).astype(lhs_ref.dtype)
    o_ref[...] = x_ref[...] + jnp.dot(lhs_ref[...], w_ref[...],
                                      preferred_element_type=jnp.float32)


def _conv_out_proj(x, attn, proj, conv_w, w_out, *, seq):
    m, d = x.shape
    aw = attn.shape[1]
    bw = conv_w.shape[1]
    tm = ROW_TILE_MIX
    halo = V7X_BF16_SUBLANES
    assert m % tm == 0 and seq % tm == 0 and CONV_WIDTH - 1 <= halo
    assert conv_w.shape == (CONV_WIDTH, bw) and CONV_WIDTH == 3
    gate0 = (proj.shape[1] - 3 * bw) // bw
    assert gate0 * bw + 3 * bw == proj.shape[1]
    halo_blocks = tm // halo

    def prev_rows(col):
        return pl.BlockSpec(
            (halo, bw), lambda i: (jnp.maximum(i * halo_blocks - 1, 0), col))

    pipelined = (2 * _nbytes((tm, d), x.dtype) + _nbytes((tm, aw), attn.dtype)
                 + 3 * _nbytes((tm, bw), proj.dtype) + _nbytes(w_out.shape, w_out.dtype))
    scratch = _nbytes((tm + halo, bw), jnp.float32) + _nbytes((tm, aw + bw), jnp.bfloat16)
    temps = 4 * _nbytes((tm, bw), jnp.float32)
    return pl.pallas_call(
        functools.partial(_conv_out_proj_kernel, rows_per_seq=seq),
        grid=(m // tm,),
        in_specs=[
            pl.BlockSpec((tm, d), lambda i: (i, 0)),
            pl.BlockSpec((tm, aw), lambda i: (i, 0)),
            pl.BlockSpec((tm, bw), lambda i: (i, gate0)),
            pl.BlockSpec((tm, bw), lambda i: (i, gate0 + 1)),
            pl.BlockSpec((tm, bw), lambda i: (i, gate0 + 2)),
            prev_rows(gate0 + 1),
            prev_rows(gate0 + 2),
            pl.BlockSpec((CONV_WIDTH, bw), lambda i: (0, 0)),
            pl.BlockSpec(w_out.shape, lambda i: (0, 0)),
        ],
        out_specs=pl.BlockSpec((tm, d), lambda i: (i, 0)),
        out_shape=jax.ShapeDtypeStruct((m, d), x.dtype),
        scratch_shapes=[pltpu.VMEM((tm + halo, bw), jnp.float32),
                        pltpu.VMEM((tm, aw + bw), jnp.bfloat16)],
        compiler_params=pltpu.CompilerParams(
            dimension_semantics=("parallel",),
            vmem_limit_bytes=_vmem_limit(pipelined, scratch, temps)),
        name="conv_out_proj",
    )(x, attn, proj, proj, proj, proj, proj, conv_w, w_out)


def _sgu_out_proj_kernel(x_ref, u_ref, v_ref, lg_ref, lb_ref, ws_ref, bs_ref, w_ref,
                         o_ref, vn_ref, lhs_ref):
    tm, cw = u_ref.shape
    gw = cw // C_GROUPS
    v = v_ref[...].astype(jnp.float32)
    mu = jnp.mean(v, axis=-1, keepdims=True)
    vc = v - mu
    var = jnp.mean(vc * vc, axis=-1, keepdims=True)
    vn_ref[...] = (vc * lax.rsqrt(var + EPS) * lg_ref[...] + lb_ref[...]).astype(vn_ref.dtype)

    t_chunk = lax.broadcasted_iota(jnp.int32, (C_BLOCK, C_BLOCK), 0) // CHUNK
    s_chunk = lax.broadcasted_iota(jnp.int32, (C_BLOCK, C_BLOCK), 1) // CHUNK
    causal = s_chunk <= t_chunk
    for g in range(C_GROUPS):
        w_m = jnp.where(causal, ws_ref[g], 0.0).astype(vn_ref.dtype)
        bias = bs_ref[:, g:g + 1]
        cols = slice(g * gw, (g + 1) * gw)
        for n in range(tm // C_BLOCK):
            rows = slice(n * C_BLOCK, (n + 1) * C_BLOCK)
            s = jnp.dot(w_m, vn_ref[rows, cols], preferred_element_type=jnp.float32) + bias
            lhs_ref[rows, cols] = (u_ref[rows, cols].astype(jnp.float32) * s
                                   ).astype(lhs_ref.dtype)
    o_ref[...] = x_ref[...] + jnp.dot(lhs_ref[...], w_ref[...],
                                      preferred_element_type=jnp.float32)


def _sgu_out_proj(x, z, ln_g, ln_b, w_s, b_s, w_out):
    m, d = x.shape
    cw = z.shape[1] // 2
    tm = ROW_TILE_MIX
    assert m % tm == 0 and tm % C_BLOCK == 0 and cw % C_GROUPS == 0
    pipelined = (2 * _nbytes((tm, d), x.dtype) + 2 * _nbytes((tm, cw), z.dtype)
                 + _nbytes(w_s.shape, w_s.dtype) + _nbytes(w_out.shape, w_out.dtype))
    scratch = 2 * _nbytes((tm, cw), jnp.bfloat16)
    temps = 3 * _nbytes((tm, cw), jnp.float32)
    return pl.pallas_call(
        _sgu_out_proj_kernel,
        grid=(m // tm,),
        in_specs=[
            pl.BlockSpec((tm, d), lambda i: (i, 0)),
            pl.BlockSpec((tm, cw), lambda i: (i, 0)),
            pl.BlockSpec((tm, cw), lambda i: (i, 1)),
            pl.BlockSpec((1, cw), lambda i: (0, 0)),
            pl.BlockSpec((1, cw), lambda i: (0, 0)),
            pl.BlockSpec(w_s.shape, lambda i: (0, 0, 0)),
            pl.BlockSpec((C_BLOCK, C_GROUPS), lambda i: (0, 0)),
            pl.BlockSpec(w_out.shape, lambda i: (0, 0)),
        ],
        out_specs=pl.BlockSpec((tm, d), lambda i: (i, 0)),
        out_shape=jax.ShapeDtypeStruct((m, d), x.dtype),
        scratch_shapes=[pltpu.VMEM((tm, cw), jnp.bfloat16),
                        pltpu.VMEM((tm, cw), jnp.bfloat16)],
        compiler_params=pltpu.CompilerParams(
            dimension_semantics=("parallel",),
            vmem_limit_bytes=_vmem_limit(pipelined, scratch, temps)),
        name="sgu_out_proj",
    )(x, z, z, ln_g.reshape(1, cw), ln_b.reshape(1, cw), w_s, jnp.transpose(b_s), w_out)


def _ffn_kernel(x_ref, g_ref, wg_ref, wu_ref, wd_ref, fg_ref, o_ref, h_ref, *,
                final_norm, casts):
    j = pl.program_id(1)

    @pl.when(j == 0)
    def _():
        x = x_ref[...]
        h_ref[...] = _rms_norm(x, g_ref[...]).astype(h_ref.dtype)
        o_ref[...] = x

    _run_casts(casts)
    h = h_ref[...]
    gate = jnp.dot(h, wg_ref[...], preferred_element_type=jnp.float32)
    up = jnp.dot(h, wu_ref[...], preferred_element_type=jnp.float32)
    act = (gate * jax.nn.sigmoid(gate) * up).astype(h.dtype)
    o_ref[...] += jnp.dot(act, wd_ref[...], preferred_element_type=jnp.float32)

    if final_norm:
        @pl.when(j == pl.num_programs(1) - 1)
        def _():
            o_ref[...] = _rms_norm(o_ref[...], fg_ref[...])


def _ffn(x, g, w_gate, w_up, w_down, final_g, *, final_norm, casts=()):
    m, d = x.shape
    f = w_gate.shape[1]
    tm, tf = ROW_TILE_PROJ, COL_TILE_FFN
    assert m % tm == 0 and f % tf == 0
    grid = (m // tm, f // tf)
    cast_in, cast_out, cast_shapes, cast_bytes = _cast_plan(casts, grid)
    out_bytes = _nbytes((tm, d), x.dtype)
    pipelined = out_bytes + 3 * _nbytes((d, tf), w_gate.dtype) + cast_bytes
    scratch = _nbytes((tm, d), jnp.bfloat16) + _nbytes((tm, d), x.dtype)
    temps = 4 * _nbytes((tm, tf), jnp.float32)
    outs = pl.pallas_call(
        _hosting_casts(functools.partial(_ffn_kernel, final_norm=final_norm), 6, 1, len(casts)),
        grid=grid,
        in_specs=[
            pl.BlockSpec((tm, d), lambda i, j: (i, 0), pipeline_mode=pl.Buffered(1)),
            pl.BlockSpec((1, d), lambda i, j: (0, 0)),
            pl.BlockSpec((d, tf), lambda i, j: (0, j)),
            pl.BlockSpec((d, tf), lambda i, j: (0, j)),
            pl.BlockSpec((tf, d), lambda i, j: (j, 0)),
            pl.BlockSpec((1, d), lambda i, j: (0, 0)),
        ] + cast_in,
        out_specs=[pl.BlockSpec((tm, d), lambda i, j: (i, 0))] + cast_out,
        out_shape=[jax.ShapeDtypeStruct((m, d), x.dtype)] + cast_shapes,
        scratch_shapes=[pltpu.VMEM((tm, d), jnp.bfloat16)],
        compiler_params=pltpu.CompilerParams(
            dimension_semantics=("arbitrary", "arbitrary"),
            vmem_limit_bytes=_vmem_limit(pipelined, scratch, temps)),
        name="ffn_final" if final_norm else "ffn",
    )(x, g.reshape(1, d), w_gate, w_up, w_down, final_g.reshape(1, d),
      *[arr for arr, _ in casts])
    return outs[0], list(outs[1:])


def kernel(x, mix_norm, ab_w_in, ab_rel_bias, ab_conv_w, ab_w_out, c_w_in, c_ln_g, c_ln_b,
           c_w_s, c_b_s, c_w_out, ffn_norm, ffn_w_gate, ffn_w_up, ffn_w_down, final_norm):
    batch, seq, d = x.shape
    depth = mix_norm.shape[0]
    heads = ab_rel_bias.shape[1]
    ffn_weights = (ffn_w_gate, ffn_w_up, ffn_w_down)

    xf = x.reshape(batch * seq, d)
    w_in = ab_w_in[0].astype(jnp.bfloat16)
    w_out = w_ffn = None
    for layer in range(depth):
        i = layer // 2
        first = layer == 0
        if layer % 2 == 0:
            proj, cast = _norm_proj(xf, mix_norm[layer], w_in, gelu=False, name="norm_proj",
                                    casts=[(ab_w_out, 0)] if first else [])
            if first:
                (w_out,) = cast
            attn, cast = _attention(proj, _bias_table(ab_rel_bias[i]), batch=batch, seq=seq,
                                    heads=heads,
                                    casts=[(w, 0) for w in ffn_weights] if first else [])
            if first:
                w_ffn = cast
            xf = _conv_out_proj(xf, attn, proj, ab_conv_w[i], w_out, seq=seq)
        else:
            z, _ = _norm_proj(xf, mix_norm[layer], w_in, gelu=True, name="norm_proj_gelu")
            xf = _sgu_out_proj(xf, z, c_ln_g[i], c_ln_b[i], c_w_s[i], c_b_s[i], w_out)
        nxt = layer + 1
        casts = []
        if nxt < depth:
            mixer = (c_w_in, c_w_out) if nxt % 2 else (ab_w_in, ab_w_out)
            casts = [(w, nxt) for w in ffn_weights] + [(w, nxt // 2) for w in mixer]
        xf, cast = _ffn(xf, ffn_norm[layer], *w_ffn, final_norm,
                        final_norm=(nxt == depth), casts=casts)
        if cast:
            w_ffn, (w_in, w_out) = cast[:3], cast[3:]
    return xf.reshape(batch, seq, d)
```

```python
import functools

import jax
import jax.numpy as jnp
import numpy as np
from jax import lax
from jax.experimental import pallas as pl
from jax.experimental.pallas import tpu as pltpu

CHUNK = 64
A_HEAD_DIM = 128
A_LEFT_CHUNKS = 8
A_MAX_REL = 256
CONV_WIDTH = 3
C_BLOCK = 128
C_GROUPS = 8
EPS = 1e-6
NEG_INF = -1e30

V7X_VMEM_BYTES = 64 * 1024 * 1024
V7X_LANES = 128
V7X_BF16_SUBLANES = 16

ROW_TILE_PROJ = 1024
ROW_TILE_MIX = 512
COL_TILE_PROJ = 2048
COL_TILE_FFN = 512
Q_BLOCK = 2 * CHUNK
K_WINDOW = Q_BLOCK + A_LEFT_CHUNKS * CHUNK
ATTN_INTERLEAVE = 8

_SQRT_HALF = np.sqrt(0.5).astype(np.float32)


def _vmem_limit(pipelined_bytes, scratch_bytes, temp_bytes):
    need = 2 * pipelined_bytes + scratch_bytes + temp_bytes
    return int(min(need + need // 8, V7X_VMEM_BYTES - 4 * 1024 * 1024))


def _nbytes(shape, dtype):
    return int(np.prod(shape)) * jnp.dtype(dtype).itemsize


def _rms_norm(x, g):
    ms = jnp.mean(x * x, axis=-1, keepdims=True)
    return (x * lax.rsqrt(ms + EPS)) * g


def _cast_plan(casts, grid):
    n_steps = int(np.prod(grid))
    strides = [int(np.prod(grid[a + 1:])) for a in range(len(grid))]
    in_specs, out_specs, out_shapes, nbytes = [], [], [], 0
    for stacked, index in casts:
        _, r, c = stacked.shape
        n_blocks = max(n for n in range(1, n_steps + 1) if r % (n * V7X_BF16_SUBLANES) == 0)
        rows = r // n_blocks

        def block_of(*ids, n_blocks=n_blocks):
            step = sum(i * s for i, s in zip(ids, strides))
            return jnp.minimum(step, n_blocks - 1)

        in_specs.append(pl.BlockSpec(
            (None, rows, c),
            lambda *ids, index=index, block_of=block_of: (index, block_of(*ids), 0)))
        out_specs.append(pl.BlockSpec(
            (rows, c), lambda *ids, block_of=block_of: (block_of(*ids), 0)))
        out_shapes.append(jax.ShapeDtypeStruct((r, c), jnp.bfloat16))
        nbytes += _nbytes((rows, c), jnp.float32) + _nbytes((rows, c), jnp.bfloat16)
    return in_specs, out_specs, out_shapes, nbytes


def _hosting_casts(body, n_in, n_out, n_cast):
    def kernel_fn(*refs):
        ins, rest = refs[:n_in], refs[n_in:]
        cast_in, rest = rest[:n_cast], rest[n_cast:]
        outs, rest = rest[:n_out], rest[n_out:]
        cast_out, scratch = rest[:n_cast], rest[n_cast:]
        body(*ins, *outs, *scratch, casts=tuple(zip(cast_in, cast_out)))
    return kernel_fn


def _run_casts(casts):
    for src, dst in casts:
        dst[...] = src[...].astype(dst.dtype)


def _norm_proj_kernel(x_ref, g_ref, w_ref, o_ref, h_ref, *, gelu, casts):
    @pl.when(pl.program_id(1) == 0)
    def _():
        h_ref[...] = _rms_norm(x_ref[...], g_ref[...]).astype(h_ref.dtype)

    _run_casts(casts)
    y = jnp.dot(h_ref[...], w_ref[...], preferred_element_type=jnp.float32)
    if gelu:
        y = 0.5 * y * (1.0 + lax.erf(y * _SQRT_HALF))
    o_ref[...] = y.astype(o_ref.dtype)


def _norm_proj(x, g, w, *, gelu, name, casts=()):
    m, d = x.shape
    n = w.shape[1]
    tm, tn = ROW_TILE_PROJ, COL_TILE_PROJ
    assert m % tm == 0 and n % tn == 0
    grid = (m // tm, n // tn)
    cast_in, cast_out, cast_shapes, cast_bytes = _cast_plan(casts, grid)
    out_dtype = jnp.bfloat16
    pipelined = (_nbytes((tm, d), x.dtype) + _nbytes((d, tn), w.dtype)
                 + _nbytes((tm, tn), out_dtype) + cast_bytes)
    scratch = _nbytes((tm, d), jnp.bfloat16)
    temps = _nbytes((tm, tn), jnp.float32)
    outs = pl.pallas_call(
        _hosting_casts(functools.partial(_norm_proj_kernel, gelu=gelu), 3, 1, len(casts)),
        grid=grid,
        in_specs=[
            pl.BlockSpec((tm, d), lambda i, j: (i, 0)),
            pl.BlockSpec((1, d), lambda i, j: (0, 0)),
            pl.BlockSpec((d, tn), lambda i, j: (0, j)),
        ] + cast_in,
        out_specs=[pl.BlockSpec((tm, tn), lambda i, j: (i, j))] + cast_out,
        out_shape=[jax.ShapeDtypeStruct((m, n), out_dtype)] + cast_shapes,
        scratch_shapes=[pltpu.VMEM((tm, d), jnp.bfloat16)],
        compiler_params=pltpu.CompilerParams(
            dimension_semantics=("arbitrary", "arbitrary"),
            vmem_limit_bytes=_vmem_limit(pipelined, scratch, temps)),
        name=name,
    )(x, g.reshape(1, d), w, *[arr for arr, _ in casts])
    return outs[0], list(outs[1:])


def _bias_table_kernel(g_ref, o_ref):
    width = Q_BLOCK + K_WINDOW
    x = jnp.broadcast_to(g_ref[0], (Q_BLOCK, width))
    row = lax.broadcasted_iota(jnp.int32, (Q_BLOCK, width), 0)
    shift = 1
    while shift < Q_BLOCK:
        x = jnp.where((row & shift) != 0, pltpu.roll(x, shift, 1), x)
        shift *= 2
    t = x[:, Q_BLOCK:]
    q_chunk = lax.broadcasted_iota(jnp.int32, (Q_BLOCK, K_WINDOW), 0) // CHUNK
    k_chunk = lax.broadcasted_iota(jnp.int32, (Q_BLOCK, K_WINDOW), 1) // CHUNK
    band = k_chunk - q_chunk
    o_ref[0] = jnp.where((band >= 0) & (band <= A_LEFT_CHUNKS), t, NEG_INF)


def _bias_table(rel_bias):
    heads, n_rel = rel_bias.shape
    width = Q_BLOCK + K_WINDOW
    left = Q_BLOCK + A_LEFT_CHUNKS * CHUNK - A_MAX_REL
    right = max(0, width - left - n_rel)
    g = jnp.pad(rel_bias[:, ::-1], ((0, 0), (left, right)), mode="edge")[:, :width]
    return pl.pallas_call(
        _bias_table_kernel,
        grid=(heads,),
        in_specs=[pl.BlockSpec((1, 1, width), lambda h: (h, 0, 0))],
        out_specs=pl.BlockSpec((1, Q_BLOCK, K_WINDOW), lambda h: (h, 0, 0)),
        out_shape=jax.ShapeDtypeStruct((heads, Q_BLOCK, K_WINDOW), jnp.float32),
        compiler_params=pltpu.CompilerParams(
            dimension_semantics=("parallel",),
            vmem_limit_bytes=_vmem_limit(
                _nbytes((Q_BLOCK, K_WINDOW), jnp.float32), 0,
                6 * _nbytes((Q_BLOCK, width), jnp.float32))),
        name="bias_table",
    )(g.reshape(heads, 1, width))


def _attention_kernel(q_ref, k_ref, v_ref, b_ref, o_ref, kt_ref, v1_ref, *, casts):
    seq, dh = q_ref.shape
    _run_casts(casts)
    kt_ref[...] = k_ref[...].T
    v1_ref[:, :dh] = v_ref[...]
    v1_ref[:, dh:] = jnp.ones((seq, dh), v1_ref.dtype)
    n_blocks = seq // Q_BLOCK
    n_groups = n_blocks // ATTN_INTERLEAVE
    for r in range(n_groups):
        win = []
        for blk in range(r, n_blocks, n_groups):
            q0 = blk * Q_BLOCK
            win.append((q0, max(0, q0 + Q_BLOCK - K_WINDOW), q0 + Q_BLOCK))
        s = [jnp.dot(q_ref[q0:k1], kt_ref[:, k0:k1], preferred_element_type=jnp.float32)
             for q0, k0, k1 in win]
        s = [si * (A_HEAD_DIM ** -0.5) + b_ref[0, :, K_WINDOW - (k1 - k0):]
             for si, (q0, k0, k1) in zip(s, win)]
        m = [jnp.max(si, axis=-1, keepdims=True) for si in s]
        p = [jnp.exp(si - mi).astype(v1_ref.dtype) for si, mi in zip(s, m)]
        o = [jnp.dot(pi, v1_ref[k0:k1], preferred_element_type=jnp.float32)
             for pi, (q0, k0, k1) in zip(p, win)]
        for oi, (q0, k0, k1) in zip(o, win):
            o_ref[q0:k1] = (oi[:, :dh] * (1.0 / oi[:, dh:dh + 1])).astype(o_ref.dtype)


def _attention(proj, bias, *, batch, seq, heads, casts=()):
    dh = A_HEAD_DIM
    assert seq % Q_BLOCK == 0 and K_WINDOW % Q_BLOCK == 0
    assert (seq // Q_BLOCK) % ATTN_INTERLEAVE == 0
    grid = (batch, heads)
    cast_in, cast_out, cast_shapes, cast_bytes = _cast_plan(casts, grid)
    pipelined = (4 * _nbytes((seq, dh), proj.dtype)
                 + _nbytes((Q_BLOCK, K_WINDOW), jnp.float32) + cast_bytes)
    scratch = 3 * _nbytes((seq, dh), proj.dtype)
    temps = 2 * ATTN_INTERLEAVE * _nbytes((Q_BLOCK, K_WINDOW), jnp.float32)
    outs = pl.pallas_call(
        _hosting_casts(_attention_kernel, 4, 1, len(casts)),
        grid=grid,
        in_specs=[
            pl.BlockSpec((seq, dh), lambda b, h: (b, h)),
            pl.BlockSpec((seq, dh), lambda b, h: (b, heads + h)),
            pl.BlockSpec((seq, dh), lambda b, h: (b, 2 * heads + h)),
            pl.BlockSpec((1, Q_BLOCK, K_WINDOW), lambda b, h: (h, 0, 0)),
        ] + cast_in,
        out_specs=[pl.BlockSpec((seq, dh), lambda b, h: (b, h))] + cast_out,
        out_shape=[jax.ShapeDtypeStruct((batch * seq, heads * dh), proj.dtype)] + cast_shapes,
        scratch_shapes=[pltpu.VMEM((dh, seq), proj.dtype),
                        pltpu.VMEM((seq, 2 * dh), proj.dtype)],
        compiler_params=pltpu.CompilerParams(
            dimension_semantics=("arbitrary", "arbitrary"),
            vmem_limit_bytes=_vmem_limit(pipelined, scratch, temps)),
        name="band_attention",
    )(proj, proj, proj, bias, *[arr for arr, _ in casts])
    return outs[0], list(outs[1:])


def _conv_out_proj_kernel(x_ref, a_ref, bg_ref, cg_ref, hv_ref, cprev_ref, hprev_ref,
                          cw_ref, w_ref, o_ref, z_ref, lhs_ref, *, rows_per_seq):
    tm, aw = a_ref.shape
    halo = cprev_ref.shape[0]
    f32 = jnp.float32
    i = pl.program_id(0)
    at_seq_start = (i * tm) % rows_per_seq == 0
    zprev = cprev_ref[...].astype(f32) * hprev_ref[...].astype(f32)
    z_ref[:halo] = jnp.where(at_seq_start, 0.0, zprev)
    z_ref[halo:] = cg_ref[...].astype(f32) * hv_ref[...].astype(f32)
    y = cw_ref[0:1] * z_ref[pl.ds(halo - 2, tm)]
    y = y + cw_ref[1:2] * z_ref[pl.ds(halo - 1, tm)]
    y = y + cw_ref[2:3] * z_ref[pl.ds(halo, tm)]
    lhs_ref[:, :aw] = a_ref[...]
    lhs_ref[:, aw:] = (bg_ref[...].astype(f32) * y).astype(lhs_ref.dtype)
    o_ref[...] = x_ref[...] + jnp.dot(lhs_ref[...], w_ref[...],
                                      preferred_element_type=jnp.float32)


def _conv_out_proj(x, attn, proj, conv_w, w_out, *, seq):
    m, d = x.shape
    aw = attn.shape[1]
    bw = conv_w.shape[1]
    tm = ROW_TILE_MIX
    halo = V7X_BF16_SUBLANES
    assert m % tm == 0 and seq % tm == 0 and CONV_WIDTH - 1 <= halo
    assert conv_w.shape == (CONV_WIDTH, bw) and CONV_WIDTH == 3
    gate0 = (proj.shape[1] - 3 * bw) // bw
    assert gate0 * bw + 3 * bw == proj.shape[1]
    halo_blocks = tm // halo

    def prev_rows(col):
        return pl.BlockSpec(
            (halo, bw), lambda i: (jnp.maximum(i * halo_blocks - 1, 0), col))

    pipelined = (2 * _nbytes((tm, d), x.dtype) + _nbytes((tm, aw), attn.dtype)
                 + 3 * _nbytes((tm, bw), proj.dtype) + _nbytes(w_out.shape, w_out.dtype))
    scratch = _nbytes((tm + halo, bw), jnp.float32) + _nbytes((tm, aw + bw), jnp.bfloat16)
    temps = 4 * _nbytes((tm, bw), jnp.float32)
    return pl.pallas_call(
        functools.partial(_conv_out_proj_kernel, rows_per_seq=seq),
        grid=(m // tm,),
        in_specs=[
            pl.BlockSpec((tm, d), lambda i: (i, 0)),
            pl.BlockSpec((tm, aw), lambda i: (i, 0)),
            pl.BlockSpec((tm, bw), lambda i: (i, gate0)),
            pl.BlockSpec((tm, bw), lambda i: (i, gate0 + 1)),
            pl.BlockSpec((tm, bw), lambda i: (i, gate0 + 2)),
            prev_rows(gate0 + 1),
            prev_rows(gate0 + 2),
            pl.BlockSpec((CONV_WIDTH, bw), lambda i: (0, 0)),
            pl.BlockSpec(w_out.shape, lambda i: (0, 0)),
        ],
        out_specs=pl.BlockSpec((tm, d), lambda i: (i, 0)),
        out_shape=jax.ShapeDtypeStruct((m, d), x.dtype),
        scratch_shapes=[pltpu.VMEM((tm + halo, bw), jnp.float32),
                        pltpu.VMEM((tm, aw + bw), jnp.bfloat16)],
        compiler_params=pltpu.CompilerParams(
            dimension_semantics=("parallel",),
            vmem_limit_bytes=_vmem_limit(pipelined, scratch, temps)),
        name="conv_out_proj",
    )(x, attn, proj, proj, proj, proj, proj, conv_w, w_out)


def _sgu_out_proj_kernel(x_ref, u_ref, v_ref, lg_ref, lb_ref, ws_ref, bs_ref, w_ref,
                         o_ref, vn_ref, lhs_ref):
    tm, cw = u_ref.shape
    gw = cw // C_GROUPS
    t_chunk = lax.broadcasted_iota(jnp.int32, (C_BLOCK, C_BLOCK), 0) // CHUNK
    s_chunk = lax.broadcasted_iota(jnp.int32, (C_BLOCK, C_BLOCK), 1) // CHUNK
    causal = s_chunk <= t_chunk
    w_m = [jnp.where(causal, ws_ref[g], 0.0).astype(vn_ref.dtype) for g in range(C_GROUPS)]

    half = tm // 2
    for r0 in range(0, tm, half):
        half_rows = slice(r0, r0 + half)
        v = v_ref[half_rows, :].astype(jnp.float32)
        mu = jnp.mean(v, axis=-1, keepdims=True)
        vc = v - mu
        var = jnp.mean(vc * vc, axis=-1, keepdims=True)
        vn_ref[half_rows, :] = (vc * lax.rsqrt(var + EPS) * lg_ref[...] + lb_ref[...]
                                ).astype(vn_ref.dtype)
        for g in range(C_GROUPS):
            bias = bs_ref[:, g:g + 1]
            cols = slice(g * gw, (g + 1) * gw)
            for n0 in range(r0, r0 + half, C_BLOCK):
                rows = slice(n0, n0 + C_BLOCK)
                s = jnp.dot(w_m[g], vn_ref[rows, cols],
                            preferred_element_type=jnp.float32) + bias
                lhs_ref[rows, cols] = (u_ref[rows, cols].astype(jnp.float32) * s
                                       ).astype(lhs_ref.dtype)
        o_ref[half_rows, :] = x_ref[half_rows, :] + jnp.dot(
            lhs_ref[half_rows, :], w_ref[...], preferred_element_type=jnp.float32)


def _sgu_out_proj(x, z, ln_g, ln_b, w_s, b_s, w_out):
    m, d = x.shape
    cw = z.shape[1] // 2
    tm = ROW_TILE_MIX
    assert m % tm == 0 and tm % C_BLOCK == 0 and cw % C_GROUPS == 0
    pipelined = (2 * _nbytes((tm, d), x.dtype) + 2 * _nbytes((tm, cw), z.dtype)
                 + _nbytes(w_s.shape, w_s.dtype) + _nbytes(w_out.shape, w_out.dtype))
    scratch = 2 * _nbytes((tm, cw), jnp.bfloat16)
    temps = 3 * _nbytes((tm, cw), jnp.float32)
    return pl.pallas_call(
        _sgu_out_proj_kernel,
        grid=(m // tm,),
        in_specs=[
            pl.BlockSpec((tm, d), lambda i: (i, 0)),
            pl.BlockSpec((tm, cw), lambda i: (i, 0)),
            pl.BlockSpec((tm, cw), lambda i: (i, 1)),
            pl.BlockSpec((1, cw), lambda i: (0, 0)),
            pl.BlockSpec((1, cw), lambda i: (0, 0)),
            pl.BlockSpec(w_s.shape, lambda i: (0, 0, 0)),
            pl.BlockSpec((C_BLOCK, C_GROUPS), lambda i: (0, 0)),
            pl.BlockSpec(w_out.shape, lambda i: (0, 0)),
        ],
        out_specs=pl.BlockSpec((tm, d), lambda i: (i, 0)),
        out_shape=jax.ShapeDtypeStruct((m, d), x.dtype),
        scratch_shapes=[pltpu.VMEM((tm, cw), jnp.bfloat16),
                        pltpu.VMEM((tm, cw), jnp.bfloat16)],
        compiler_params=pltpu.CompilerParams(
            dimension_semantics=("parallel",),
            vmem_limit_bytes=_vmem_limit(pipelined, scratch, temps)),
        name="sgu_out_proj",
    )(x, z, z, ln_g.reshape(1, cw), ln_b.reshape(1, cw), w_s, jnp.transpose(b_s), w_out)


def _ffn_kernel(x_ref, g_ref, wg_ref, wu_ref, wd_ref, fg_ref, o_ref, h_ref, *,
                final_norm, casts):
    j = pl.program_id(1)

    @pl.when(j == 0)
    def _():
        x = x_ref[...]
        h_ref[...] = _rms_norm(x, g_ref[...]).astype(h_ref.dtype)
        o_ref[...] = x

    _run_casts(casts)
    h = h_ref[...]
    tf = wg_ref.shape[1]
    acts = []
    for c0 in range(0, tf, tf // 2):
        cols = slice(c0, c0 + tf // 2)
        gate = jnp.dot(h, wg_ref[:, cols], preferred_element_type=jnp.float32)
        up = jnp.dot(h, wu_ref[:, cols], preferred_element_type=jnp.float32)
        acts.append((gate * jax.nn.sigmoid(gate) * up).astype(h.dtype))
    act = jnp.concatenate(acts, axis=1)
    o_ref[...] += jnp.dot(act, wd_ref[...], preferred_element_type=jnp.float32)

    if final_norm:
        @pl.when(j == pl.num_programs(1) - 1)
        def _():
            o_ref[...] = _rms_norm(o_ref[...], fg_ref[...])


def _ffn(x, g, w_gate, w_up, w_down, final_g, *, final_norm, casts=()):
    m, d = x.shape
    f = w_gate.shape[1]
    tm, tf = ROW_TILE_PROJ, COL_TILE_FFN
    assert m % tm == 0 and f % tf == 0
    grid = (m // tm, f // tf)
    cast_in, cast_out, cast_shapes, cast_bytes = _cast_plan(casts, grid)
    out_bytes = _nbytes((tm, d), x.dtype)
    pipelined = out_bytes + 3 * _nbytes((d, tf), w_gate.dtype) + cast_bytes
    scratch = _nbytes((tm, d), jnp.bfloat16) + _nbytes((tm, d), x.dtype)
    temps = 4 * _nbytes((tm, tf), jnp.float32)
    outs = pl.pallas_call(
        _hosting_casts(functools.partial(_ffn_kernel, final_norm=final_norm), 6, 1, len(casts)),
        grid=grid,
        in_specs=[
            pl.BlockSpec((tm, d), lambda i, j: (i, 0), pipeline_mode=pl.Buffered(1)),
            pl.BlockSpec((1, d), lambda i, j: (0, 0)),
            pl.BlockSpec((d, tf), lambda i, j: (0, j)),
            pl.BlockSpec((d, tf), lambda i, j: (0, j)),
            pl.BlockSpec((tf, d), lambda i, j: (j, 0)),
            pl.BlockSpec((1, d), lambda i, j: (0, 0)),
        ] + cast_in,
        out_specs=[pl.BlockSpec((tm, d), lambda i, j: (i, 0))] + cast_out,
        out_shape=[jax.ShapeDtypeStruct((m, d), x.dtype)] + cast_shapes,
        scratch_shapes=[pltpu.VMEM((tm, d), jnp.bfloat16)],
        compiler_params=pltpu.CompilerParams(
            dimension_semantics=("arbitrary", "arbitrary"),
            vmem_limit_bytes=_vmem_limit(pipelined, scratch, temps)),
        name="ffn_final" if final_norm else "ffn",
    )(x, g.reshape(1, d), w_gate, w_up, w_down, final_g.reshape(1, d),
      *[arr for arr, _ in casts])
    return outs[0], list(outs[1:])


def kernel(x, mix_norm, ab_w_in, ab_rel_bias, ab_conv_w, ab_w_out, c_w_in, c_ln_g, c_ln_b,
           c_w_s, c_b_s, c_w_out, ffn_norm, ffn_w_gate, ffn_w_up, ffn_w_down, final_norm):
    batch, seq, d = x.shape
    depth = mix_norm.shape[0]
    heads = ab_rel_bias.shape[1]
    ffn_weights = (ffn_w_gate, ffn_w_up, ffn_w_down)

    xf = x.reshape(batch * seq, d)
    w_in = ab_w_in[0].astype(jnp.bfloat16)
    w_out = w_ffn = None
    for layer in range(depth):
        i = layer // 2
        first = layer == 0
        if layer % 2 == 0:
            proj, cast = _norm_proj(xf, mix_norm[layer], w_in, gelu=False, name="norm_proj",
                                    casts=[(ab_w_out, 0)] if first else [])
            if first:
                (w_out,) = cast
            attn, cast = _attention(proj, _bias_table(ab_rel_bias[i]), batch=batch, seq=seq,
                                    heads=heads,
                                    casts=[(w, 0) for w in ffn_weights] if first else [])
            if first:
                w_ffn = cast
            xf = _conv_out_proj(xf, attn, proj, ab_conv_w[i], w_out, seq=seq)
        else:
            z, _ = _norm_proj(xf, mix_norm[layer], w_in, gelu=True, name="norm_proj_gelu")
            xf = _sgu_out_proj(xf, z, c_ln_g[i], c_ln_b[i], c_w_s[i], c_b_s[i], w_out)
        nxt = layer + 1
        casts = []
        if nxt < depth:
            mixer = (c_w_in, c_w_out) if nxt % 2 else (ab_w_in, ab_w_out)
            casts = [(w, nxt) for w in ffn_weights] + [(w, nxt // 2) for w in mixer]
        xf, cast = _ffn(xf, ffn_norm[layer], *w_ffn, final_norm,
                        final_norm=(nxt == depth), casts=casts)
        if cast:
            w_ffn, (w_in, w_out) = cast[:3], cast[3:]
    return xf.reshape(batch, seq, d)
```

```python
import functools

import jax
import jax.numpy as jnp
import numpy as np
from jax import lax
from jax.experimental import pallas as pl
from jax.experimental.pallas import tpu as pltpu

CHUNK = 64
A_HEAD_DIM = 128
A_LEFT_CHUNKS = 8
A_MAX_REL = 256
CONV_WIDTH = 3
C_BLOCK = 128
C_GROUPS = 8
EPS = 1e-6
NEG_INF = -1e30

V7X_VMEM_BYTES = 64 * 1024 * 1024
V7X_BF16_SUBLANES = 16

ROW_TILE_PROJ = 1024
ROW_TILE_MIX = 512
COL_TILE_PROJ = 1024
COL_TILE_FFN = 512
Q_BLOCK = 2 * CHUNK
K_WINDOW = Q_BLOCK + A_LEFT_CHUNKS * CHUNK
ATTN_INTERLEAVE = 8

_SQRT_HALF = np.sqrt(0.5).astype(np.float32)
_BF16 = jnp.bfloat16


def _vmem_limit(pipelined_bytes, single_bytes, temp_bytes):
    need = 2 * pipelined_bytes + single_bytes + temp_bytes
    return int(min(need + need // 8, V7X_VMEM_BYTES - 4 * 1024 * 1024))


def _nbytes(shape, dtype):
    return int(np.prod(shape)) * jnp.dtype(dtype).itemsize


def _rms_norm(x, g):
    ms = jnp.mean(x * x, axis=-1, keepdims=True)
    return (x * lax.rsqrt(ms + EPS)) * g


def _norm_proj_kernel(x_ref, g_ref, w_ref, o_ref, h_ref, *, gelu):
    @pl.when(pl.program_id(1) == 0)
    def _():
        h_ref[...] = _rms_norm(x_ref[...], g_ref[...]).astype(h_ref.dtype)

    y = jnp.dot(h_ref[...], w_ref[...].astype(h_ref.dtype),
                preferred_element_type=jnp.float32)
    if gelu:
        y = 0.5 * y * (1.0 + lax.erf(y * _SQRT_HALF))
    o_ref[...] = y.astype(o_ref.dtype)


def _norm_proj(x, g, w, index, *, gelu, name):
    m, d = x.shape
    n = w.shape[2]
    tm, tn = ROW_TILE_PROJ, COL_TILE_PROJ
    assert m % tm == 0 and n % tn == 0
    pipelined = (_nbytes((tm, d), x.dtype) + _nbytes((d, tn), w.dtype)
                 + _nbytes((tm, tn), _BF16))
    scratch = _nbytes((tm, d), _BF16)
    temps = _nbytes((d, tn), _BF16) + _nbytes((tm, tn), jnp.float32)
    return pl.pallas_call(
        functools.partial(_norm_proj_kernel, gelu=gelu),
        grid=(m // tm, n // tn),
        in_specs=[
            pl.BlockSpec((tm, d), lambda i, j: (i, 0)),
            pl.BlockSpec((1, d), lambda i, j: (0, 0)),
            pl.BlockSpec((None, d, tn), lambda i, j: (index, 0, j)),
        ],
        out_specs=pl.BlockSpec((tm, tn), lambda i, j: (i, j)),
        out_shape=jax.ShapeDtypeStruct((m, n), _BF16),
        scratch_shapes=[pltpu.VMEM((tm, d), _BF16)],
        compiler_params=pltpu.CompilerParams(
            dimension_semantics=("parallel", "arbitrary"),
            vmem_limit_bytes=_vmem_limit(pipelined, scratch, temps)),
        name=name,
    )(x, g.reshape(1, d), w)


def _bias_table_kernel(g_ref, o_ref):
    width = Q_BLOCK + K_WINDOW
    x = jnp.broadcast_to(g_ref[0], (Q_BLOCK, width))
    row = lax.broadcasted_iota(jnp.int32, (Q_BLOCK, width), 0)
    shift = 1
    while shift < Q_BLOCK:
        x = jnp.where((row & shift) != 0, pltpu.roll(x, shift, 1), x)
        shift *= 2
    t = x[:, Q_BLOCK:]
    q_chunk = lax.broadcasted_iota(jnp.int32, (Q_BLOCK, K_WINDOW), 0) // CHUNK
    k_chunk = lax.broadcasted_iota(jnp.int32, (Q_BLOCK, K_WINDOW), 1) // CHUNK
    band = k_chunk - q_chunk
    o_ref[0] = jnp.where((band >= 0) & (band <= A_LEFT_CHUNKS), t, NEG_INF)


def _bias_table(rel_bias):
    heads, n_rel = rel_bias.shape
    width = Q_BLOCK + K_WINDOW
    left = Q_BLOCK + A_LEFT_CHUNKS * CHUNK - A_MAX_REL
    right = max(0, width - left - n_rel)
    g = jnp.pad(rel_bias[:, ::-1], ((0, 0), (left, right)), mode="edge")[:, :width]
    return pl.pallas_call(
        _bias_table_kernel,
        grid=(heads,),
        in_specs=[pl.BlockSpec((1, 1, width), lambda h: (h, 0, 0))],
        out_specs=pl.BlockSpec((1, Q_BLOCK, K_WINDOW), lambda h: (h, 0, 0)),
        out_shape=jax.ShapeDtypeStruct((heads, Q_BLOCK, K_WINDOW), jnp.float32),
        compiler_params=pltpu.CompilerParams(
            dimension_semantics=("parallel",),
            vmem_limit_bytes=_vmem_limit(
                _nbytes((Q_BLOCK, K_WINDOW), jnp.float32), 0,
                6 * _nbytes((Q_BLOCK, width), jnp.float32))),
        name="bias_table",
    )(g.reshape(heads, 1, width))


def _attention_kernel(q_ref, k_ref, v_ref, b_ref, o_ref, kt_ref, v1_ref):
    seq, dh = q_ref.shape
    kt_ref[...] = k_ref[...].T
    v1_ref[:, :dh] = v_ref[...]
    v1_ref[:, dh:] = jnp.ones((seq, dh), v1_ref.dtype)
    n_blocks = seq // Q_BLOCK
    n_groups = n_blocks // ATTN_INTERLEAVE
    for r in range(n_groups):
        win = []
        for blk in range(r, n_blocks, n_groups):
            q0 = blk * Q_BLOCK
            win.append((q0, max(0, q0 + Q_BLOCK - K_WINDOW), q0 + Q_BLOCK))
        s = [jnp.dot(q_ref[q0:k1], kt_ref[:, k0:k1], preferred_element_type=jnp.float32)
             for q0, k0, k1 in win]
        s = [si * (A_HEAD_DIM ** -0.5) + b_ref[0, :, K_WINDOW - (k1 - k0):]
             for si, (q0, k0, k1) in zip(s, win)]
        m = [jnp.max(si, axis=-1, keepdims=True) for si in s]
        p = [jnp.exp(si - mi).astype(v1_ref.dtype) for si, mi in zip(s, m)]
        o = [jnp.dot(pi, v1_ref[k0:k1], preferred_element_type=jnp.float32)
             for pi, (q0, k0, k1) in zip(p, win)]
        for oi, (q0, k0, k1) in zip(o, win):
            o_ref[q0:k1] = (oi[:, :dh] * (1.0 / oi[:, dh:dh + 1])).astype(o_ref.dtype)


def _attention(proj, bias, *, batch, seq, heads):
    dh = A_HEAD_DIM
    assert seq % Q_BLOCK == 0 and K_WINDOW % Q_BLOCK == 0
    assert (seq // Q_BLOCK) % ATTN_INTERLEAVE == 0
    pipelined = (4 * _nbytes((seq, dh), proj.dtype)
                 + _nbytes((Q_BLOCK, K_WINDOW), jnp.float32))
    scratch = 3 * _nbytes((seq, dh), proj.dtype)
    temps = 2 * ATTN_INTERLEAVE * _nbytes((Q_BLOCK, K_WINDOW), jnp.float32)
    return pl.pallas_call(
        _attention_kernel,
        grid=(batch, heads),
        in_specs=[
            pl.BlockSpec((seq, dh), lambda b, h: (b, h)),
            pl.BlockSpec((seq, dh), lambda b, h: (b, heads + h)),
            pl.BlockSpec((seq, dh), lambda b, h: (b, 2 * heads + h)),
            pl.BlockSpec((1, Q_BLOCK, K_WINDOW), lambda b, h: (h, 0, 0)),
        ],
        out_specs=pl.BlockSpec((seq, dh), lambda b, h: (b, h)),
        out_shape=jax.ShapeDtypeStruct((batch * seq, heads * dh), proj.dtype),
        scratch_shapes=[pltpu.VMEM((dh, seq), proj.dtype),
                        pltpu.VMEM((seq, 2 * dh), proj.dtype)],
        compiler_params=pltpu.CompilerParams(
            dimension_semantics=("parallel", "parallel"),
            vmem_limit_bytes=_vmem_limit(pipelined, scratch, temps)),
        name="band_attention",
    )(proj, proj, proj, bias)


def _conv_out_proj_kernel(x_ref, a_ref, bg_ref, cg_ref, hv_ref, cprev_ref, hprev_ref,
                          cw_ref, w_ref, o_ref, z_ref, lhs_ref, *, rows_per_seq):
    tm, aw = a_ref.shape
    halo = cprev_ref.shape[0]
    f32 = jnp.float32
    i = pl.program_id(0)
    at_seq_start = (i * tm) % rows_per_seq == 0
    zprev = cprev_ref[...].astype(f32) * hprev_ref[...].astype(f32)
    z_ref[:halo] = jnp.where(at_seq_start, 0.0, zprev)
    z_ref[halo:] = cg_ref[...].astype(f32) * hv_ref[...].astype(f32)
    y = cw_ref[0:1] * z_ref[pl.ds(halo - 2, tm)]
    y = y + cw_ref[1:2] * z_ref[pl.ds(halo - 1, tm)]
    y = y + cw_ref[2:3] * z_ref[pl.ds(halo, tm)]
    lhs_ref[:, :aw] = a_ref[...]
    lhs_ref[:, aw:] = (bg_ref[...].astype(f32) * y).astype(lhs_ref.dtype)
    o_ref[...] = x_ref[...] + jnp.dot(lhs_ref[...], w_ref[...].astype(lhs_ref.dtype),
                                      preferred_element_type=jnp.float32)


def _conv_out_proj(x, attn, proj, conv_w, w_out, index, *, seq):
    m, d = x.shape
    aw = attn.shape[1]
    bw = conv_w.shape[1]
    tm = ROW_TILE_MIX
    halo = V7X_BF16_SUBLANES
    assert m % tm == 0 and seq % tm == 0 and CONV_WIDTH - 1 <= halo
    assert conv_w.shape == (CONV_WIDTH, bw) and CONV_WIDTH == 3
    gate0 = (proj.shape[1] - 3 * bw) // bw
    assert gate0 * bw + 3 * bw == proj.shape[1]
    halo_blocks = tm // halo
    w_shape = w_out.shape[1:]

    def prev_rows(col):
        return pl.BlockSpec(
            (halo, bw), lambda i: (jnp.maximum(i * halo_blocks - 1, 0), col))

    pipelined = (2 * _nbytes((tm, d), x.dtype) + _nbytes((tm, aw), attn.dtype)
                 + 3 * _nbytes((tm, bw), proj.dtype))
    single = (_nbytes(w_shape, w_out.dtype) + _nbytes((tm + halo, bw), jnp.float32)
              + _nbytes((tm, aw + bw), _BF16))
    temps = _nbytes(w_shape, _BF16) + 2 * _nbytes((tm, bw), jnp.float32)
    return pl.pallas_call(
        functools.partial(_conv_out_proj_kernel, rows_per_seq=seq),
        grid=(m // tm,),
        in_specs=[
            pl.BlockSpec((tm, d), lambda i: (i, 0)),
            pl.BlockSpec((tm, aw), lambda i: (i, 0)),
            pl.BlockSpec((tm, bw), lambda i: (i, gate0)),
            pl.BlockSpec((tm, bw), lambda i: (i, gate0 + 1)),
            pl.BlockSpec((tm, bw), lambda i: (i, gate0 + 2)),
            prev_rows(gate0 + 1),
            prev_rows(gate0 + 2),
            pl.BlockSpec((CONV_WIDTH, bw), lambda i: (0, 0)),
            pl.BlockSpec((None,) + w_shape, lambda i: (index, 0, 0),
                         pipeline_mode=pl.Buffered(1)),
        ],
        out_specs=pl.BlockSpec((tm, d), lambda i: (i, 0)),
        out_shape=jax.ShapeDtypeStruct((m, d), x.dtype),
        scratch_shapes=[pltpu.VMEM((tm + halo, bw), jnp.float32),
                        pltpu.VMEM((tm, aw + bw), _BF16)],
        compiler_params=pltpu.CompilerParams(
            dimension_semantics=("parallel",),
            vmem_limit_bytes=_vmem_limit(pipelined, single, temps)),
        name="conv_out_proj",
    )(x, attn, proj, proj, proj, proj, proj, conv_w, w_out)


def _sgu_out_proj_kernel(x_ref, u_ref, v_ref, lg_ref, lb_ref, ws_ref, bs_ref, w_ref,
                         o_ref, vn_ref, lhs_ref):
    tm, cw = u_ref.shape
    gw = cw // C_GROUPS
    t_chunk = lax.broadcasted_iota(jnp.int32, (C_BLOCK, C_BLOCK), 0) // CHUNK
    s_chunk = lax.broadcasted_iota(jnp.int32, (C_BLOCK, C_BLOCK), 1) // CHUNK
    causal = s_chunk <= t_chunk
    w_m = [jnp.where(causal, ws_ref[g], 0.0).astype(vn_ref.dtype) for g in range(C_GROUPS)]
    w_out = w_ref[...].astype(lhs_ref.dtype)

    half = tm // 2
    for r0 in range(0, tm, half):
        half_rows = slice(r0, r0 + half)
        v = v_ref[half_rows, :].astype(jnp.float32)
        mu = jnp.mean(v, axis=-1, keepdims=True)
        vc = v - mu
        var = jnp.mean(vc * vc, axis=-1, keepdims=True)
        vn_ref[half_rows, :] = (vc * lax.rsqrt(var + EPS) * lg_ref[...] + lb_ref[...]
                                ).astype(vn_ref.dtype)
        for g in range(C_GROUPS):
            bias = bs_ref[:, g:g + 1]
            cols = slice(g * gw, (g + 1) * gw)
            for n0 in range(r0, r0 + half, C_BLOCK):
                rows = slice(n0, n0 + C_BLOCK)
                s = jnp.dot(w_m[g], vn_ref[rows, cols],
                            preferred_element_type=jnp.float32) + bias
                lhs_ref[rows, cols] = (u_ref[rows, cols].astype(jnp.float32) * s
                                       ).astype(lhs_ref.dtype)
        o_ref[half_rows, :] = x_ref[half_rows, :] + jnp.dot(
            lhs_ref[half_rows, :], w_out, preferred_element_type=jnp.float32)


def _sgu_out_proj(x, z, ln_g, ln_b, w_s, b_s, w_out, index):
    m, d = x.shape
    cw = z.shape[1] // 2
    tm = ROW_TILE_MIX
    assert m % tm == 0 and tm % (2 * C_BLOCK) == 0 and cw % C_GROUPS == 0
    w_shape = w_out.shape[1:]
    pipelined = (2 * _nbytes((tm, d), x.dtype) + 2 * _nbytes((tm, cw), z.dtype)
                 + _nbytes(w_s.shape, w_s.dtype))
    single = _nbytes(w_shape, w_out.dtype) + 2 * _nbytes((tm, cw), _BF16)
    temps = _nbytes(w_shape, _BF16) + 2 * _nbytes((tm // 2, cw), jnp.float32)
    return pl.pallas_call(
        _sgu_out_proj_kernel,
        grid=(m // tm,),
        in_specs=[
            pl.BlockSpec((tm, d), lambda i: (i, 0)),
            pl.BlockSpec((tm, cw), lambda i: (i, 0)),
            pl.BlockSpec((tm, cw), lambda i: (i, 1)),
            pl.BlockSpec((1, cw), lambda i: (0, 0)),
            pl.BlockSpec((1, cw), lambda i: (0, 0)),
            pl.BlockSpec(w_s.shape, lambda i: (0, 0, 0)),
            pl.BlockSpec((C_BLOCK, C_GROUPS), lambda i: (0, 0)),
            pl.BlockSpec((None,) + w_shape, lambda i: (index, 0, 0),
                         pipeline_mode=pl.Buffered(1)),
        ],
        out_specs=pl.BlockSpec((tm, d), lambda i: (i, 0)),
        out_shape=jax.ShapeDtypeStruct((m, d), x.dtype),
        scratch_shapes=[pltpu.VMEM((tm, cw), _BF16), pltpu.VMEM((tm, cw), _BF16)],
        compiler_params=pltpu.CompilerParams(
            dimension_semantics=("parallel",),
            vmem_limit_bytes=_vmem_limit(pipelined, single, temps)),
        name="sgu_out_proj",
    )(x, z, z, ln_g.reshape(1, cw), ln_b.reshape(1, cw), w_s, jnp.transpose(b_s), w_out)


def _ffn_kernel(x_ref, g_ref, wg_ref, wu_ref, wd_ref, fg_ref, o_ref, h_ref, *, final_norm):
    j = pl.program_id(1)

    @pl.when(j == 0)
    def _():
        x = x_ref[...]
        h_ref[...] = _rms_norm(x, g_ref[...]).astype(h_ref.dtype)
        o_ref[...] = x

    h = h_ref[...]
    tf = wg_ref.shape[1]
    acts = []
    for c0 in range(0, tf, tf // 2):
        cols = slice(c0, c0 + tf // 2)
        gate = jnp.dot(h, wg_ref[:, cols].astype(h.dtype), preferred_element_type=jnp.float32)
        up = jnp.dot(h, wu_ref[:, cols].astype(h.dtype), preferred_element_type=jnp.float32)
        acts.append((gate * jax.nn.sigmoid(gate) * up).astype(h.dtype))
    act = jnp.concatenate(acts, axis=1)
    o_ref[...] += jnp.dot(act, wd_ref[...].astype(h.dtype), preferred_element_type=jnp.float32)

    if final_norm:
        @pl.when(j == pl.num_programs(1) - 1)
        def _():
            o_ref[...] = _rms_norm(o_ref[...], fg_ref[...])


def _ffn(x, g, w_gate, w_up, w_down, index, final_g, *, final_norm):
    m, d = x.shape
    f = w_gate.shape[2]
    tm, tf = ROW_TILE_PROJ, COL_TILE_FFN
    assert m % tm == 0 and f % tf == 0
    pipelined = _nbytes((tm, d), x.dtype) + 3 * _nbytes((d, tf), w_gate.dtype)
    single = _nbytes((tm, d), x.dtype) + _nbytes((tm, d), _BF16)
    temps = 2 * _nbytes((d, tf), _BF16) + 2 * _nbytes((tm, tf), jnp.float32)
    return pl.pallas_call(
        functools.partial(_ffn_kernel, final_norm=final_norm),
        grid=(m // tm, f // tf),
        in_specs=[
            pl.BlockSpec((tm, d), lambda i, j: (i, 0), pipeline_mode=pl.Buffered(1)),
            pl.BlockSpec((1, d), lambda i, j: (0, 0)),
            pl.BlockSpec((None, d, tf), lambda i, j: (index, 0, j)),
            pl.BlockSpec((None, d, tf), lambda i, j: (index, 0, j)),
            pl.BlockSpec((None, tf, d), lambda i, j: (index, j, 0)),
            pl.BlockSpec((1, d), lambda i, j: (0, 0)),
        ],
        out_specs=pl.BlockSpec((tm, d), lambda i, j: (i, 0)),
        out_shape=jax.ShapeDtypeStruct((m, d), x.dtype),
        scratch_shapes=[pltpu.VMEM((tm, d), _BF16)],
        compiler_params=pltpu.CompilerParams(
            dimension_semantics=("parallel", "arbitrary"),
            vmem_limit_bytes=_vmem_limit(pipelined, single, temps)),
        name="ffn_final" if final_norm else "ffn",
    )(x, g.reshape(1, d), w_gate, w_up, w_down, final_g.reshape(1, d))


def kernel(x, mix_norm, ab_w_in, ab_rel_bias, ab_conv_w, ab_w_out, c_w_in, c_ln_g, c_ln_b,
           c_w_s, c_b_s, c_w_out, ffn_norm, ffn_w_gate, ffn_w_up, ffn_w_down, final_norm):
    batch, seq, d = x.shape
    depth = mix_norm.shape[0]
    heads = ab_rel_bias.shape[1]

    xf = x.reshape(batch * seq, d)
    for layer in range(depth):
        i = layer // 2
        if layer % 2 == 0:
            proj = _norm_proj(xf, mix_norm[layer], ab_w_in, i, gelu=False, name="norm_proj")
            attn = _attention(proj, _bias_table(ab_rel_bias[i]), batch=batch, seq=seq,
                              heads=heads)
            xf = _conv_out_proj(xf, attn, proj, ab_conv_w[i], ab_w_out, i, seq=seq)
        else:
            z = _norm_proj(xf, mix_norm[layer], c_w_in, i, gelu=True, name="norm_proj_gelu")
            xf = _sgu_out_proj(xf, z, c_ln_g[i], c_ln_b[i], c_w_s[i], c_b_s[i], c_w_out, i)
        xf = _ffn(xf, ffn_norm[layer], ffn_w_gate, ffn_w_up, ffn_w_down, layer, final_norm,
                  final_norm=(layer == depth - 1))
    return xf.reshape(batch, seq, d)
```

```python
import functools

import jax
import jax.numpy as jnp
import numpy as np
from jax import lax
from jax.experimental import pallas as pl
from jax.experimental.pallas import tpu as pltpu

CHUNK = 64
A_HEAD_DIM = 128
A_LEFT_CHUNKS = 8
A_MAX_REL = 256
CONV_WIDTH = 3
C_BLOCK = 128
C_GROUPS = 8
EPS = 1e-6
NEG_INF = -1e30

V7X_VMEM_BYTES = 64 * 1024 * 1024
V7X_BF16_SUBLANES = 16

ROW_TILE_PROJ = 1024
ROW_TILE_MIX = 512
COL_TILE_PROJ = 2048
COL_TILE_FFN = 512
Q_BLOCK = 2 * CHUNK
K_WINDOW = Q_BLOCK + A_LEFT_CHUNKS * CHUNK
ATTN_INTERLEAVE = 8

_SQRT_HALF = np.sqrt(0.5).astype(np.float32)
_BF16 = jnp.bfloat16


def _vmem_limit(pipelined_bytes, single_bytes, temp_bytes):
    need = 2 * pipelined_bytes + single_bytes + temp_bytes
    return int(min(need + need // 8, V7X_VMEM_BYTES - 4 * 1024 * 1024))


def _nbytes(shape, dtype):
    return int(np.prod(shape)) * jnp.dtype(dtype).itemsize


def _rms_norm(x, g):
    ms = jnp.mean(x * x, axis=-1, keepdims=True)
    return (x * lax.rsqrt(ms + EPS)) * g


def _norm_proj_kernel(*refs, gelu, has_next_w):
    if has_next_w:
        x_ref, g_ref, w_ref, next_w_ref, o_ref, next_w16_ref, h_ref = refs
    else:
        x_ref, g_ref, w_ref, o_ref, h_ref = refs

    @pl.when(pl.program_id(1) == 0)
    def _():
        h_ref[...] = _rms_norm(x_ref[...], g_ref[...]).astype(h_ref.dtype)

    if has_next_w:
        next_w16_ref[...] = next_w_ref[...].astype(next_w16_ref.dtype)
    y = jnp.dot(h_ref[...], w_ref[...], preferred_element_type=jnp.float32)
    if gelu:
        y = 0.5 * y * (1.0 + lax.erf(y * _SQRT_HALF))
    o_ref[...] = y.astype(o_ref.dtype)


def _norm_proj(x, g, w, *, gelu, name, next_w=None):
    m, d = x.shape
    n = w.shape[1]
    tm, tn = ROW_TILE_PROJ, COL_TILE_PROJ
    assert m % tm == 0 and n % tn == 0
    grid = (m // tm, n // tn)
    in_specs = [
        pl.BlockSpec((tm, d), lambda i, j: (i, 0)),
        pl.BlockSpec((1, d), lambda i, j: (0, 0)),
        pl.BlockSpec((d, tn), lambda i, j: (0, j)),
    ]
    out_specs = [pl.BlockSpec((tm, tn), lambda i, j: (i, j))]
    out_shape = [jax.ShapeDtypeStruct((m, n), _BF16)]
    operands = [x, g.reshape(1, d), w]
    cast_bytes = 0
    if next_w is not None:
        stacked, index = next_w
        _, r, c = stacked.shape
        n_steps = grid[0] * grid[1]
        n_blocks = max(k for k in range(1, n_steps + 1) if r % (k * V7X_BF16_SUBLANES) == 0)
        rows = r // n_blocks

        def block_of(i, j):
            return jnp.minimum(i * grid[1] + j, n_blocks - 1)

        in_specs.append(pl.BlockSpec((None, rows, c), lambda i, j: (index, block_of(i, j), 0)))
        out_specs.append(pl.BlockSpec((rows, c), lambda i, j: (block_of(i, j), 0)))
        out_shape.append(jax.ShapeDtypeStruct((r, c), _BF16))
        operands.append(stacked)
        cast_bytes = _nbytes((rows, c), stacked.dtype) + _nbytes((rows, c), _BF16)
    pipelined = (_nbytes((tm, d), x.dtype) + _nbytes((d, tn), w.dtype)
                 + _nbytes((tm, tn), _BF16) + cast_bytes)
    scratch = _nbytes((tm, d), _BF16)
    temps = _nbytes((tm, tn), jnp.float32)
    outs = pl.pallas_call(
        functools.partial(_norm_proj_kernel, gelu=gelu, has_next_w=next_w is not None),
        grid=grid,
        in_specs=in_specs,
        out_specs=out_specs,
        out_shape=out_shape,
        scratch_shapes=[pltpu.VMEM((tm, d), _BF16)],
        compiler_params=pltpu.CompilerParams(
            dimension_semantics=("arbitrary", "arbitrary"),
            vmem_limit_bytes=_vmem_limit(pipelined, scratch, temps)),
        name=name,
    )(*operands)
    return outs[0], (outs[1] if next_w is not None else None)


def _bias_table_kernel(g_ref, o_ref):
    width = Q_BLOCK + K_WINDOW
    x = jnp.broadcast_to(g_ref[0], (Q_BLOCK, width))
    row = lax.broadcasted_iota(jnp.int32, (Q_BLOCK, width), 0)
    shift = 1
    while shift < Q_BLOCK:
        x = jnp.where((row & shift) != 0, pltpu.roll(x, shift, 1), x)
        shift *= 2
    t = x[:, Q_BLOCK:]
    q_chunk = lax.broadcasted_iota(jnp.int32, (Q_BLOCK, K_WINDOW), 0) // CHUNK
    k_chunk = lax.broadcasted_iota(jnp.int32, (Q_BLOCK, K_WINDOW), 1) // CHUNK
    band = k_chunk - q_chunk
    o_ref[0] = jnp.where((band >= 0) & (band <= A_LEFT_CHUNKS), t, NEG_INF)


def _bias_table(rel_bias):
    heads, n_rel = rel_bias.shape
    width = Q_BLOCK + K_WINDOW
    left = Q_BLOCK + A_LEFT_CHUNKS * CHUNK - A_MAX_REL
    right = max(0, width - left - n_rel)
    g = jnp.pad(rel_bias[:, ::-1], ((0, 0), (left, right)), mode="edge")[:, :width]
    return pl.pallas_call(
        _bias_table_kernel,
        grid=(heads,),
        in_specs=[pl.BlockSpec((1, 1, width), lambda h: (h, 0, 0))],
        out_specs=pl.BlockSpec((1, Q_BLOCK, K_WINDOW), lambda h: (h, 0, 0)),
        out_shape=jax.ShapeDtypeStruct((heads, Q_BLOCK, K_WINDOW), jnp.float32),
        compiler_params=pltpu.CompilerParams(
            dimension_semantics=("parallel",),
            vmem_limit_bytes=_vmem_limit(
                _nbytes((Q_BLOCK, K_WINDOW), jnp.float32), 0,
                6 * _nbytes((Q_BLOCK, width), jnp.float32))),
        name="bias_table",
    )(g.reshape(heads, 1, width))


def _attention_kernel(q_ref, k_ref, v_ref, b_ref, o_ref, kt_ref, v1_ref):
    seq, dh = q_ref.shape
    kt_ref[...] = k_ref[...].T
    v1_ref[:, :dh] = v_ref[...]
    v1_ref[:, dh:] = jnp.ones((seq, dh), v1_ref.dtype)
    n_blocks = seq // Q_BLOCK
    n_groups = n_blocks // ATTN_INTERLEAVE
    for r in range(n_groups):
        win = []
        for blk in range(r, n_blocks, n_groups):
            q0 = blk * Q_BLOCK
            win.append((q0, max(0, q0 + Q_BLOCK - K_WINDOW), q0 + Q_BLOCK))
        s = [jnp.dot(q_ref[q0:k1], kt_ref[:, k0:k1], preferred_element_type=jnp.float32)
             for q0, k0, k1 in win]
        s = [si * (A_HEAD_DIM ** -0.5) + b_ref[0, :, K_WINDOW - (k1 - k0):]
             for si, (q0, k0, k1) in zip(s, win)]
        m = [jnp.max(si, axis=-1, keepdims=True) for si in s]
        p = [jnp.exp(si - mi).astype(v1_ref.dtype) for si, mi in zip(s, m)]
        o = [jnp.dot(pi, v1_ref[k0:k1], preferred_element_type=jnp.float32)
             for pi, (q0, k0, k1) in zip(p, win)]
        for oi, (q0, k0, k1) in zip(o, win):
            o_ref[q0:k1] = (oi[:, :dh] * (1.0 / oi[:, dh:dh + 1])).astype(o_ref.dtype)


def _attention(proj, bias, *, batch, seq, heads):
    dh = A_HEAD_DIM
    assert seq % Q_BLOCK == 0 and K_WINDOW % Q_BLOCK == 0
    assert (seq // Q_BLOCK) % ATTN_INTERLEAVE == 0
    pipelined = (4 * _nbytes((seq, dh), proj.dtype)
                 + _nbytes((Q_BLOCK, K_WINDOW), jnp.float32))
    scratch = 3 * _nbytes((seq, dh), proj.dtype)
    temps = 2 * ATTN_INTERLEAVE * _nbytes((Q_BLOCK, K_WINDOW), jnp.float32)
    return pl.pallas_call(
        _attention_kernel,
        grid=(batch, heads),
        in_specs=[
            pl.BlockSpec((seq, dh), lambda b, h: (b, h)),
            pl.BlockSpec((seq, dh), lambda b, h: (b, heads + h)),
            pl.BlockSpec((seq, dh), lambda b, h: (b, 2 * heads + h)),
            pl.BlockSpec((1, Q_BLOCK, K_WINDOW), lambda b, h: (h, 0, 0)),
        ],
        out_specs=pl.BlockSpec((seq, dh), lambda b, h: (b, h)),
        out_shape=jax.ShapeDtypeStruct((batch * seq, heads * dh), proj.dtype),
        scratch_shapes=[pltpu.VMEM((dh, seq), proj.dtype),
                        pltpu.VMEM((seq, 2 * dh), proj.dtype)],
        compiler_params=pltpu.CompilerParams(
            dimension_semantics=("parallel", "parallel"),
            vmem_limit_bytes=_vmem_limit(pipelined, scratch, temps)),
        name="band_attention",
    )(proj, proj, proj, bias)


def _conv_out_proj_kernel(x_ref, a_ref, bg_ref, cg_ref, hv_ref, cprev_ref, hprev_ref,
                          cw_ref, w_ref, o_ref, z_ref, lhs_ref, *, rows_per_seq):
    tm, aw = a_ref.shape
    halo = cprev_ref.shape[0]
    f32 = jnp.float32
    i = pl.program_id(0)
    at_seq_start = (i * tm) % rows_per_seq == 0
    zprev = cprev_ref[...].astype(f32) * hprev_ref[...].astype(f32)
    z_ref[:halo] = jnp.where(at_seq_start, 0.0, zprev)
    z_ref[halo:] = cg_ref[...].astype(f32) * hv_ref[...].astype(f32)
    y = cw_ref[0:1] * z_ref[pl.ds(halo - 2, tm)]
    y = y + cw_ref[1:2] * z_ref[pl.ds(halo - 1, tm)]
    y = y + cw_ref[2:3] * z_ref[pl.ds(halo, tm)]
    lhs_ref[:, :aw] = a_ref[...]
    lhs_ref[:, aw:] = (bg_ref[...].astype(f32) * y).astype(lhs_ref.dtype)
    o_ref[...] = x_ref[...] + jnp.dot(lhs_ref[...], w_ref[...].astype(lhs_ref.dtype),
                                      preferred_element_type=jnp.float32)


def _conv_out_proj(x, attn, proj, conv_w, w_out, index, *, seq):
    m, d = x.shape
    aw = attn.shape[1]
    bw = conv_w.shape[1]
    tm = ROW_TILE_MIX
    halo = V7X_BF16_SUBLANES
    assert m % tm == 0 and seq % tm == 0 and CONV_WIDTH - 1 <= halo
    assert conv_w.shape == (CONV_WIDTH, bw) and CONV_WIDTH == 3
    gate0 = (proj.shape[1] - 3 * bw) // bw
    assert gate0 * bw + 3 * bw == proj.shape[1]
    halo_blocks = tm // halo
    w_shape = w_out.shape[1:]

    def prev_rows(col):
        return pl.BlockSpec(
            (halo, bw), lambda i: (jnp.maximum(i * halo_blocks - 1, 0), col))

    pipelined = (2 * _nbytes((tm, d), x.dtype) + _nbytes((tm, aw), attn.dtype)
                 + 3 * _nbytes((tm, bw), proj.dtype))
    single = (_nbytes(w_shape, w_out.dtype) + _nbytes((tm + halo, bw), jnp.float32)
              + _nbytes((tm, aw + bw), _BF16))
    temps = _nbytes(w_shape, _BF16) + 2 * _nbytes((tm, bw), jnp.float32)
    return pl.pallas_call(
        functools.partial(_conv_out_proj_kernel, rows_per_seq=seq),
        grid=(m // tm,),
        in_specs=[
            pl.BlockSpec((tm, d), lambda i: (i, 0)),
            pl.BlockSpec((tm, aw), lambda i: (i, 0)),
            pl.BlockSpec((tm, bw), lambda i: (i, gate0)),
            pl.BlockSpec((tm, bw), lambda i: (i, gate0 + 1)),
            pl.BlockSpec((tm, bw), lambda i: (i, gate0 + 2)),
            prev_rows(gate0 + 1),
            prev_rows(gate0 + 2),
            pl.BlockSpec((CONV_WIDTH, bw), lambda i: (0, 0)),
            pl.BlockSpec((None,) + w_shape, lambda i: (index, 0, 0),
                         pipeline_mode=pl.Buffered(1)),
        ],
        out_specs=pl.BlockSpec((tm, d), lambda i: (i, 0)),
        out_shape=jax.ShapeDtypeStruct((m, d), x.dtype),
        scratch_shapes=[pltpu.VMEM((tm + halo, bw), jnp.float32),
                        pltpu.VMEM((tm, aw + bw), _BF16)],
        compiler_params=pltpu.CompilerParams(
            dimension_semantics=("parallel",),
            vmem_limit_bytes=_vmem_limit(pipelined, single, temps)),
        name="conv_out_proj",
    )(x, attn, proj, proj, proj, proj, proj, conv_w, w_out)


def _sgu_out_proj_kernel(x_ref, u_ref, v_ref, lg_ref, lb_ref, ws_ref, bs_ref, w_ref,
                         o_ref, vn_ref, lhs_ref):
    tm, cw = u_ref.shape
    gw = cw // C_GROUPS
    t_chunk = lax.broadcasted_iota(jnp.int32, (C_BLOCK, C_BLOCK), 0) // CHUNK
    s_chunk = lax.broadcasted_iota(jnp.int32, (C_BLOCK, C_BLOCK), 1) // CHUNK
    causal = s_chunk <= t_chunk
    w_m = [jnp.where(causal, ws_ref[g], 0.0).astype(vn_ref.dtype) for g in range(C_GROUPS)]
    w_out = w_ref[...].astype(lhs_ref.dtype)

    half = tm // 2
    for r0 in range(0, tm, half):
        half_rows = slice(r0, r0 + half)
        v = v_ref[half_rows, :].astype(jnp.float32)
        mu = jnp.mean(v, axis=-1, keepdims=True)
        vc = v - mu
        var = jnp.mean(vc * vc, axis=-1, keepdims=True)
        vn_ref[half_rows, :] = (vc * lax.rsqrt(var + EPS) * lg_ref[...] + lb_ref[...]
                                ).astype(vn_ref.dtype)
        for g in range(C_GROUPS):
            bias = bs_ref[:, g:g + 1]
            cols = slice(g * gw, (g + 1) * gw)
            for n0 in range(r0, r0 + half, C_BLOCK):
                rows = slice(n0, n0 + C_BLOCK)
                s = jnp.dot(w_m[g], vn_ref[rows, cols],
                            preferred_element_type=jnp.float32) + bias
                lhs_ref[rows, cols] = (u_ref[rows, cols].astype(jnp.float32) * s
                                       ).astype(lhs_ref.dtype)
        o_ref[half_rows, :] = x_ref[half_rows, :] + jnp.dot(
            lhs_ref[half_rows, :], w_out, preferred_element_type=jnp.float32)


def _sgu_out_proj(x, z, ln_g, ln_b, w_s, b_s, w_out, index):
    m, d = x.shape
    cw = z.shape[1] // 2
    tm = ROW_TILE_MIX
    assert m % tm == 0 and tm % (2 * C_BLOCK) == 0 and cw % C_GROUPS == 0
    w_shape = w_out.shape[1:]
    pipelined = (2 * _nbytes((tm, d), x.dtype) + 2 * _nbytes((tm, cw), z.dtype)
                 + _nbytes(w_s.shape, w_s.dtype))
    single = _nbytes(w_shape, w_out.dtype) + 2 * _nbytes((tm, cw), _BF16)
    temps = _nbytes(w_shape, _BF16) + 2 * _nbytes((tm // 2, cw), jnp.float32)
    return pl.pallas_call(
        _sgu_out_proj_kernel,
        grid=(m // tm,),
        in_specs=[
            pl.BlockSpec((tm, d), lambda i: (i, 0)),
            pl.BlockSpec((tm, cw), lambda i: (i, 0)),
            pl.BlockSpec((tm, cw), lambda i: (i, 1)),
            pl.BlockSpec((1, cw), lambda i: (0, 0)),
            pl.BlockSpec((1, cw), lambda i: (0, 0)),
            pl.BlockSpec(w_s.shape, lambda i: (0, 0, 0)),
            pl.BlockSpec((C_BLOCK, C_GROUPS), lambda i: (0, 0)),
            pl.BlockSpec((None,) + w_shape, lambda i: (index, 0, 0),
                         pipeline_mode=pl.Buffered(1)),
        ],
        out_specs=pl.BlockSpec((tm, d), lambda i: (i, 0)),
        out_shape=jax.ShapeDtypeStruct((m, d), x.dtype),
        scratch_shapes=[pltpu.VMEM((tm, cw), _BF16), pltpu.VMEM((tm, cw), _BF16)],
        compiler_params=pltpu.CompilerParams(
            dimension_semantics=("parallel",),
            vmem_limit_bytes=_vmem_limit(pipelined, single, temps)),
        name="sgu_out_proj",
    )(x, z, z, ln_g.reshape(1, cw), ln_b.reshape(1, cw), w_s, jnp.transpose(b_s), w_out)


def _ffn_kernel(x_ref, g_ref, wg_ref, wu_ref, wd_ref, fg_ref, o_ref, h_ref, *, final_norm):
    j = pl.program_id(1)

    @pl.when(j == 0)
    def _():
        x = x_ref[...]
        h_ref[...] = _rms_norm(x, g_ref[...]).astype(h_ref.dtype)
        o_ref[...] = x

    h = h_ref[...]
    tf = wg_ref.shape[1]
    acts = []
    for c0 in range(0, tf, tf // 2):
        cols = slice(c0, c0 + tf // 2)
        gate = jnp.dot(h, wg_ref[:, cols].astype(h.dtype), preferred_element_type=jnp.float32)
        up = jnp.dot(h, wu_ref[:, cols].astype(h.dtype), preferred_element_type=jnp.float32)
        acts.append((gate * jax.nn.sigmoid(gate) * up).astype(h.dtype))
    act = jnp.concatenate(acts, axis=1)
    o_ref[...] += jnp.dot(act, wd_ref[...].astype(h.dtype), preferred_element_type=jnp.float32)

    if final_norm:
        @pl.when(j == pl.num_programs(1) - 1)
        def _():
            o_ref[...] = _rms_norm(o_ref[...], fg_ref[...])


def _ffn(x, g, w_gate, w_up, w_down, index, final_g, *, final_norm):
    m, d = x.shape
    f = w_gate.shape[2]
    tm, tf = ROW_TILE_PROJ, COL_TILE_FFN
    assert m % tm == 0 and f % tf == 0
    pipelined = _nbytes((tm, d), x.dtype) + 3 * _nbytes((d, tf), w_gate.dtype)
    single = _nbytes((tm, d), x.dtype) + _nbytes((tm, d), _BF16)
    temps = 2 * _nbytes((d, tf), _BF16) + 2 * _nbytes((tm, tf), jnp.float32)
    return pl.pallas_call(
        functools.partial(_ffn_kernel, final_norm=final_norm),
        grid=(m // tm, f // tf),
        in_specs=[
            pl.BlockSpec((tm, d), lambda i, j: (i, 0), pipeline_mode=pl.Buffered(1)),
            pl.BlockSpec((1, d), lambda i, j: (0, 0)),
            pl.BlockSpec((None, d, tf), lambda i, j: (index, 0, j)),
            pl.BlockSpec((None, d, tf), lambda i, j: (index, 0, j)),
            pl.BlockSpec((None, tf, d), lambda i, j: (index, j, 0)),
            pl.BlockSpec((1, d), lambda i, j: (0, 0)),
        ],
        out_specs=pl.BlockSpec((tm, d), lambda i, j: (i, 0)),
        out_shape=jax.ShapeDtypeStruct((m, d), x.dtype),
        scratch_shapes=[pltpu.VMEM((tm, d), _BF16)],
        compiler_params=pltpu.CompilerParams(
            dimension_semantics=("parallel", "arbitrary"),
            vmem_limit_bytes=_vmem_limit(pipelined, single, temps)),
        name="ffn_final" if final_norm else "ffn",
    )(x, g.reshape(1, d), w_gate, w_up, w_down, final_g.reshape(1, d))


def kernel(x, mix_norm, ab_w_in, ab_rel_bias, ab_conv_w, ab_w_out, c_w_in, c_ln_g, c_ln_b,
           c_w_s, c_b_s, c_w_out, ffn_norm, ffn_w_gate, ffn_w_up, ffn_w_down, final_norm):
    batch, seq, d = x.shape
    depth = mix_norm.shape[0]
    heads = ab_rel_bias.shape[1]

    xf = x.reshape(batch * seq, d)
    w_in = ab_w_in[0].astype(_BF16)
    for layer in range(depth):
        i = layer // 2
        nxt = layer + 1
        next_w = None
        if nxt < depth:
            next_w = (c_w_in if nxt % 2 else ab_w_in, nxt // 2)
        if layer % 2 == 0:
            proj, w_in = _norm_proj(xf, mix_norm[layer], w_in, gelu=False, name="norm_proj",
                                    next_w=next_w)
            attn = _attention(proj, _bias_table(ab_rel_bias[i]), batch=batch, seq=seq,
                              heads=heads)
            xf = _conv_out_proj(xf, attn, proj, ab_conv_w[i], ab_w_out, i, seq=seq)
        else:
            z, w_in = _norm_proj(xf, mix_norm[layer], w_in, gelu=True, name="norm_proj_gelu",
                                 next_w=next_w)
            xf = _sgu_out_proj(xf, z, c_ln_g[i], c_ln_b[i], c_w_s[i], c_b_s[i], c_w_out, i)
        xf = _ffn(xf, ffn_norm[layer], ffn_w_gate, ffn_w_up, ffn_w_down, layer, final_norm,
                  final_norm=(layer == depth - 1))
    return xf.reshape(batch, seq, d)
```

```python
import functools

import jax
import jax.numpy as jnp
import numpy as np
from jax import lax
from jax.experimental import pallas as pl
from jax.experimental.pallas import tpu as pltpu

CHUNK = 64
A_HEAD_DIM = 128
A_LEFT_CHUNKS = 8
A_MAX_REL = 256
CONV_WIDTH = 3
C_BLOCK = 128
C_GROUPS = 8
EPS = 1e-6
NEG_INF = -1e30

V7X_VMEM_BYTES = 64 * 1024 * 1024
V7X_BF16_SUBLANES = 16

ROW_TILE_PROJ = 1024
ROW_TILE_MIX = 512
COL_TILE_PROJ = 2048
COL_TILE_FFN = 512
Q_BLOCK = 2 * CHUNK
K_WINDOW = Q_BLOCK + A_LEFT_CHUNKS * CHUNK
ATTN_INTERLEAVE = 8
ATTN_HEADS = 4

_SQRT_HALF = np.sqrt(0.5).astype(np.float32)
_BF16 = jnp.bfloat16


def _vmem_limit(pipelined_bytes, single_bytes, temp_bytes):
    need = 2 * pipelined_bytes + single_bytes + temp_bytes
    return int(min(need + need // 8, V7X_VMEM_BYTES - 4 * 1024 * 1024))


def _nbytes(shape, dtype):
    return int(np.prod(shape)) * jnp.dtype(dtype).itemsize


def _rms_norm(x, g):
    ms = jnp.mean(x * x, axis=-1, keepdims=True)
    return (x * lax.rsqrt(ms + EPS)) * g


def _norm_proj_kernel(*refs, gelu, has_next_w):
    if has_next_w:
        x_ref, g_ref, w_ref, next_w_ref, o_ref, next_w16_ref, h_ref = refs
    else:
        x_ref, g_ref, w_ref, o_ref, h_ref = refs

    def project(h):
        if has_next_w:
            next_w16_ref[...] = next_w_ref[...].astype(next_w16_ref.dtype)
        y = jnp.dot(h, w_ref[...], preferred_element_type=jnp.float32)
        if gelu:
            y = 0.5 * y * (1.0 + lax.erf(y * _SQRT_HALF))
        o_ref[...] = y.astype(o_ref.dtype)

    j = pl.program_id(1)

    @pl.when(j == 0)
    def _():
        h = _rms_norm(x_ref[...], g_ref[...]).astype(h_ref.dtype)
        h_ref[...] = h
        project(h)

    @pl.when(j > 0)
    def _():
        project(h_ref[...])


def _norm_proj(x, g, w, *, gelu, name, next_w=None):
    m, d = x.shape
    n = w.shape[1]
    tm, tn = ROW_TILE_PROJ, COL_TILE_PROJ
    assert m % tm == 0 and n % tn == 0
    grid = (m // tm, n // tn)
    in_specs = [
        pl.BlockSpec((tm, d), lambda i, j: (i, 0)),
        pl.BlockSpec((1, d), lambda i, j: (0, 0)),
        pl.BlockSpec((d, tn), lambda i, j: (0, j)),
    ]
    out_specs = [pl.BlockSpec((tm, tn), lambda i, j: (i, j))]
    out_shape = [jax.ShapeDtypeStruct((m, n), _BF16)]
    operands = [x, g.reshape(1, d), w]
    cast_bytes = 0
    if next_w is not None:
        stacked, index = next_w
        _, r, c = stacked.shape
        n_steps = grid[0] * grid[1]
        n_blocks = max(k for k in range(1, n_steps + 1) if r % (k * V7X_BF16_SUBLANES) == 0)
        rows = r // n_blocks

        def block_of(i, j):
            return jnp.minimum(i * grid[1] + j, n_blocks - 1)

        in_specs.append(pl.BlockSpec((None, rows, c), lambda i, j: (index, block_of(i, j), 0)))
        out_specs.append(pl.BlockSpec((rows, c), lambda i, j: (block_of(i, j), 0)))
        out_shape.append(jax.ShapeDtypeStruct((r, c), _BF16))
        operands.append(stacked)
        cast_bytes = _nbytes((rows, c), stacked.dtype) + _nbytes((rows, c), _BF16)
    pipelined = (_nbytes((tm, d), x.dtype) + _nbytes((d, tn), w.dtype)
                 + _nbytes((tm, tn), _BF16) + cast_bytes)
    scratch = _nbytes((tm, d), _BF16)
    temps = _nbytes((tm, tn), jnp.float32)
    outs = pl.pallas_call(
        functools.partial(_norm_proj_kernel, gelu=gelu, has_next_w=next_w is not None),
        grid=grid,
        in_specs=in_specs,
        out_specs=out_specs,
        out_shape=out_shape,
        scratch_shapes=[pltpu.VMEM((tm, d), _BF16)],
        compiler_params=pltpu.CompilerParams(
            dimension_semantics=("arbitrary", "arbitrary"),
            vmem_limit_bytes=_vmem_limit(pipelined, scratch, temps)),
        name=name,
    )(*operands)
    return outs[0], (outs[1] if next_w is not None else None)


def _bias_table_kernel(g_ref, o_ref):
    width = Q_BLOCK + K_WINDOW
    x = jnp.broadcast_to(g_ref[0], (Q_BLOCK, width))
    row = lax.broadcasted_iota(jnp.int32, (Q_BLOCK, width), 0)
    shift = 1
    while shift < Q_BLOCK:
        x = jnp.where((row & shift) != 0, pltpu.roll(x, shift, 1), x)
        shift *= 2
    t = x[:, Q_BLOCK:]
    q_chunk = lax.broadcasted_iota(jnp.int32, (Q_BLOCK, K_WINDOW), 0) // CHUNK
    k_chunk = lax.broadcasted_iota(jnp.int32, (Q_BLOCK, K_WINDOW), 1) // CHUNK
    band = k_chunk - q_chunk
    o_ref[0] = jnp.where((band >= 0) & (band <= A_LEFT_CHUNKS), t, NEG_INF)


def _bias_table(rel_bias):
    heads, n_rel = rel_bias.shape
    width = Q_BLOCK + K_WINDOW
    left = Q_BLOCK + A_LEFT_CHUNKS * CHUNK - A_MAX_REL
    right = max(0, width - left - n_rel)
    g = jnp.pad(rel_bias[:, ::-1], ((0, 0), (left, right)), mode="edge")[:, :width]
    return pl.pallas_call(
        _bias_table_kernel,
        grid=(heads,),
        in_specs=[pl.BlockSpec((1, 1, width), lambda h: (h, 0, 0))],
        out_specs=pl.BlockSpec((1, Q_BLOCK, K_WINDOW), lambda h: (h, 0, 0)),
        out_shape=jax.ShapeDtypeStruct((heads, Q_BLOCK, K_WINDOW), jnp.float32),
        compiler_params=pltpu.CompilerParams(
            dimension_semantics=("parallel",),
            vmem_limit_bytes=_vmem_limit(
                _nbytes((Q_BLOCK, K_WINDOW), jnp.float32), 0,
                6 * _nbytes((Q_BLOCK, width), jnp.float32))),
        name="bias_table",
    )(g.reshape(heads, 1, width))


def _attention_kernel(q_ref, k_ref, v_ref, b_ref, o_ref, kt_ref, v1_ref):
    seq = q_ref.shape[0]
    dh = A_HEAD_DIM
    kt_ref[...] = k_ref[...].T
    n_blocks = seq // Q_BLOCK
    n_groups = n_blocks // ATTN_INTERLEAVE
    for hd in range(ATTN_HEADS):
        cols = slice(hd * dh, (hd + 1) * dh)
        v1 = v1_ref.at[hd]
        v1[:, :dh] = v_ref[:, cols]
        v1[:, dh:] = jnp.ones((seq, dh), v1_ref.dtype)
        for r in range(n_groups):
            win = []
            for blk in range(r, n_blocks, n_groups):
                q0 = blk * Q_BLOCK
                win.append((q0, max(0, q0 + Q_BLOCK - K_WINDOW), q0 + Q_BLOCK))
            s = [jnp.dot(q_ref[q0:k1, cols], kt_ref[cols, k0:k1],
                         preferred_element_type=jnp.float32) for q0, k0, k1 in win]
            s = [si * (dh ** -0.5) + b_ref[hd, :, K_WINDOW - (k1 - k0):]
                 for si, (q0, k0, k1) in zip(s, win)]
            m = [jnp.max(si, axis=-1, keepdims=True) for si in s]
            p = [jnp.exp(si - mi).astype(v1_ref.dtype) for si, mi in zip(s, m)]
            o = [jnp.dot(pi, v1[k0:k1], preferred_element_type=jnp.float32)
                 for pi, (q0, k0, k1) in zip(p, win)]
            for oi, (q0, k0, k1) in zip(o, win):
                o_ref[q0:k1, cols] = (oi[:, :dh] * (1.0 / oi[:, dh:dh + 1])).astype(o_ref.dtype)


def _attention(proj, bias, table0, *, batch, seq, heads):
    dh = A_HEAD_DIM
    hs = ATTN_HEADS
    assert seq % Q_BLOCK == 0 and K_WINDOW % Q_BLOCK == 0
    assert (seq // Q_BLOCK) % ATTN_INTERLEAVE == 0 and heads % hs == 0 and table0 % hs == 0
    n_hb = heads // hs
    pipelined = (4 * _nbytes((seq, hs * dh), proj.dtype)
                 + _nbytes((hs, Q_BLOCK, K_WINDOW), jnp.float32))
    scratch = 3 * _nbytes((seq, hs * dh), proj.dtype)
    temps = 2 * ATTN_INTERLEAVE * _nbytes((Q_BLOCK, K_WINDOW), jnp.float32)
    return pl.pallas_call(
        _attention_kernel,
        grid=(batch, n_hb),
        in_specs=[
            pl.BlockSpec((seq, hs * dh), lambda b, h: (b, h)),
            pl.BlockSpec((seq, hs * dh), lambda b, h: (b, n_hb + h)),
            pl.BlockSpec((seq, hs * dh), lambda b, h: (b, 2 * n_hb + h)),
            pl.BlockSpec((hs, Q_BLOCK, K_WINDOW), lambda b, h: (table0 // hs + h, 0, 0)),
        ],
        out_specs=pl.BlockSpec((seq, hs * dh), lambda b, h: (b, h)),
        out_shape=jax.ShapeDtypeStruct((batch * seq, heads * dh), proj.dtype),
        scratch_shapes=[pltpu.VMEM((hs * dh, seq), proj.dtype),
                        pltpu.VMEM((hs, seq, 2 * dh), proj.dtype)],
        compiler_params=pltpu.CompilerParams(
            dimension_semantics=("parallel", "parallel"),
            vmem_limit_bytes=_vmem_limit(pipelined, scratch, temps)),
        name="band_attention",
    )(proj, proj, proj, bias)


def _conv_out_proj_kernel(x_ref, a_ref, bg_ref, cg_ref, hv_ref, cprev_ref, hprev_ref,
                          cw_ref, w_ref, o_ref, z_ref, lhs_ref, *, rows_per_seq):
    tm, aw = a_ref.shape
    halo = cprev_ref.shape[0]
    f32 = jnp.float32
    i = pl.program_id(0)
    at_seq_start = (i * tm) % rows_per_seq == 0
    zprev = cprev_ref[...].astype(f32) * hprev_ref[...].astype(f32)
    z_ref[:halo] = jnp.where(at_seq_start, 0.0, zprev)
    z_ref[halo:] = cg_ref[...].astype(f32) * hv_ref[...].astype(f32)
    y = cw_ref[0:1] * z_ref[pl.ds(halo - 2, tm)]
    y = y + cw_ref[1:2] * z_ref[pl.ds(halo - 1, tm)]
    y = y + cw_ref[2:3] * z_ref[pl.ds(halo, tm)]
    lhs_ref[:, :aw] = a_ref[...]
    lhs_ref[:, aw:] = (bg_ref[...].astype(f32) * y).astype(lhs_ref.dtype)
    o_ref[...] = x_ref[...] + jnp.dot(lhs_ref[...], w_ref[...].astype(lhs_ref.dtype),
                                      preferred_element_type=jnp.float32)


def _conv_out_proj(x, attn, proj, conv_w, w_out, index, *, seq):
    m, d = x.shape
    aw = attn.shape[1]
    bw = conv_w.shape[1]
    tm = ROW_TILE_MIX
    halo = V7X_BF16_SUBLANES
    assert m % tm == 0 and seq % tm == 0 and CONV_WIDTH - 1 <= halo
    assert conv_w.shape == (CONV_WIDTH, bw) and CONV_WIDTH == 3
    gate0 = (proj.shape[1] - 3 * bw) // bw
    assert gate0 * bw + 3 * bw == proj.shape[1]
    halo_blocks = tm // halo
    w_shape = w_out.shape[1:]

    def prev_rows(col):
        return pl.BlockSpec(
            (halo, bw), lambda i: (jnp.maximum(i * halo_blocks - 1, 0), col))

    pipelined = (2 * _nbytes((tm, d), x.dtype) + _nbytes((tm, aw), attn.dtype)
                 + 3 * _nbytes((tm, bw), proj.dtype))
    single = (_nbytes(w_shape, w_out.dtype) + _nbytes((tm + halo, bw), jnp.float32)
              + _nbytes((tm, aw + bw), _BF16))
    temps = _nbytes(w_shape, _BF16) + 2 * _nbytes((tm, bw), jnp.float32)
    return pl.pallas_call(
        functools.partial(_conv_out_proj_kernel, rows_per_seq=seq),
        grid=(m // tm,),
        in_specs=[
            pl.BlockSpec((tm, d), lambda i: (i, 0)),
            pl.BlockSpec((tm, aw), lambda i: (i, 0)),
            pl.BlockSpec((tm, bw), lambda i: (i, gate0)),
            pl.BlockSpec((tm, bw), lambda i: (i, gate0 + 1)),
            pl.BlockSpec((tm, bw), lambda i: (i, gate0 + 2)),
            prev_rows(gate0 + 1),
            prev_rows(gate0 + 2),
            pl.BlockSpec((CONV_WIDTH, bw), lambda i: (0, 0)),
            pl.BlockSpec((None,) + w_shape, lambda i: (index, 0, 0),
                         pipeline_mode=pl.Buffered(1)),
        ],
        out_specs=pl.BlockSpec((tm, d), lambda i: (i, 0)),
        out_shape=jax.ShapeDtypeStruct((m, d), x.dtype),
        scratch_shapes=[pltpu.VMEM((tm + halo, bw), jnp.float32),
                        pltpu.VMEM((tm, aw + bw), _BF16)],
        compiler_params=pltpu.CompilerParams(
            dimension_semantics=("parallel",),
            vmem_limit_bytes=_vmem_limit(pipelined, single, temps)),
        name="conv_out_proj",
    )(x, attn, proj, proj, proj, proj, proj, conv_w, w_out)


def _sgu_out_proj_kernel(x_ref, u_ref, v_ref, lg_ref, lb_ref, ws_ref, bs_ref, w_ref,
                         o_ref, vn_ref, lhs_ref):
    tm, cw = u_ref.shape
    gw = cw // C_GROUPS
    t_chunk = lax.broadcasted_iota(jnp.int32, (C_BLOCK, C_BLOCK), 0) // CHUNK
    s_chunk = lax.broadcasted_iota(jnp.int32, (C_BLOCK, C_BLOCK), 1) // CHUNK
    causal = s_chunk <= t_chunk
    w_m = [jnp.where(causal, ws_ref[g], 0.0).astype(vn_ref.dtype) for g in range(C_GROUPS)]
    w_out = w_ref[...].astype(lhs_ref.dtype)

    half = tm // 2
    for r0 in range(0, tm, half):
        half_rows = slice(r0, r0 + half)
        v = v_ref[half_rows, :].astype(jnp.float32)
        mu = jnp.mean(v, axis=-1, keepdims=True)
        vc = v - mu
        var = jnp.mean(vc * vc, axis=-1, keepdims=True)
        vn_ref[half_rows, :] = (vc * lax.rsqrt(var + EPS) * lg_ref[...] + lb_ref[...]
                                ).astype(vn_ref.dtype)
        for g in range(C_GROUPS):
            bias = bs_ref[:, g:g + 1]
            cols = slice(g * gw, (g + 1) * gw)
            for n0 in range(r0, r0 + half, C_BLOCK):
                rows = slice(n0, n0 + C_BLOCK)
                s = jnp.dot(w_m[g], vn_ref[rows, cols],
                            preferred_element_type=jnp.float32) + bias
                lhs_ref[rows, cols] = (u_ref[rows, cols].astype(jnp.float32) * s
                                       ).astype(lhs_ref.dtype)
        o_ref[half_rows, :] = x_ref[half_rows, :] + jnp.dot(
            lhs_ref[half_rows, :], w_out, preferred_element_type=jnp.float32)


def _sgu_out_proj(x, z, ln_g, ln_b, w_s, b_s, w_out, index):
    m, d = x.shape
    cw = z.shape[1] // 2
    tm = ROW_TILE_MIX
    assert m % tm == 0 and tm % (2 * C_BLOCK) == 0 and cw % C_GROUPS == 0
    w_shape = w_out.shape[1:]
    pipelined = (2 * _nbytes((tm, d), x.dtype) + 2 * _nbytes((tm, cw), z.dtype)
                 + _nbytes(w_s.shape, w_s.dtype))
    single = _nbytes(w_shape, w_out.dtype) + 2 * _nbytes((tm, cw), _BF16)
    temps = _nbytes(w_shape, _BF16) + 2 * _nbytes((tm // 2, cw), jnp.float32)
    return pl.pallas_call(
        _sgu_out_proj_kernel,
        grid=(m // tm,),
        in_specs=[
            pl.BlockSpec((tm, d), lambda i: (i, 0)),
            pl.BlockSpec((tm, cw), lambda i: (i, 0)),
            pl.BlockSpec((tm, cw), lambda i: (i, 1)),
            pl.BlockSpec((1, cw), lambda i: (0, 0)),
            pl.BlockSpec((1, cw), lambda i: (0, 0)),
            pl.BlockSpec(w_s.shape, lambda i: (0, 0, 0)),
            pl.BlockSpec((C_BLOCK, C_GROUPS), lambda i: (0, 0)),
            pl.BlockSpec((None,) + w_shape, lambda i: (index, 0, 0),
                         pipeline_mode=pl.Buffered(1)),
        ],
        out_specs=pl.BlockSpec((tm, d), lambda i: (i, 0)),
        out_shape=jax.ShapeDtypeStruct((m, d), x.dtype),
        scratch_shapes=[pltpu.VMEM((tm, cw), _BF16), pltpu.VMEM((tm, cw), _BF16)],
        compiler_params=pltpu.CompilerParams(
            dimension_semantics=("parallel",),
            vmem_limit_bytes=_vmem_limit(pipelined, single, temps)),
        name="sgu_out_proj",
    )(x, z, z, ln_g.reshape(1, cw), ln_b.reshape(1, cw), w_s, jnp.transpose(b_s), w_out)


def _ffn_kernel(x_ref, g_ref, wg_ref, wu_ref, wd_ref, fg_ref, o_ref, h_ref, *, final_norm):
    j = pl.program_id(1)
    tf = wg_ref.shape[1]

    def swiglu_tile(h):
        acts = []
        for c0 in range(0, tf, tf // 2):
            cols = slice(c0, c0 + tf // 2)
            gate = jnp.dot(h, wg_ref[:, cols].astype(h.dtype),
                           preferred_element_type=jnp.float32)
            up = jnp.dot(h, wu_ref[:, cols].astype(h.dtype),
                         preferred_element_type=jnp.float32)
            acts.append((gate * jax.nn.sigmoid(gate) * up).astype(h.dtype))
        act = jnp.concatenate(acts, axis=1)
        return jnp.dot(act, wd_ref[...].astype(h.dtype), preferred_element_type=jnp.float32)

    @pl.when(j == 0)
    def _():
        x = x_ref[...]
        h = _rms_norm(x, g_ref[...]).astype(h_ref.dtype)
        h_ref[...] = h
        o_ref[...] = x + swiglu_tile(h)

    @pl.when(j > 0)
    def _():
        o_ref[...] += swiglu_tile(h_ref[...])

    if final_norm:
        @pl.when(j == pl.num_programs(1) - 1)
        def _():
            o_ref[...] = _rms_norm(o_ref[...], fg_ref[...])


def _ffn(x, g, w_gate, w_up, w_down, index, final_g, *, final_norm):
    m, d = x.shape
    f = w_gate.shape[2]
    tm, tf = ROW_TILE_PROJ, COL_TILE_FFN
    assert m % tm == 0 and f % tf == 0
    pipelined = _nbytes((tm, d), x.dtype) + 3 * _nbytes((d, tf), w_gate.dtype)
    single = _nbytes((tm, d), x.dtype) + _nbytes((tm, d), _BF16)
    temps = 2 * _nbytes((d, tf), _BF16) + 2 * _nbytes((tm, tf), jnp.float32)
    return pl.pallas_call(
        functools.partial(_ffn_kernel, final_norm=final_norm),
        grid=(m // tm, f // tf),
        in_specs=[
            pl.BlockSpec((tm, d), lambda i, j: (i, 0), pipeline_mode=pl.Buffered(1)),
            pl.BlockSpec((1, d), lambda i, j: (0, 0)),
            pl.BlockSpec((None, d, tf), lambda i, j: (index, 0, j)),
            pl.BlockSpec((None, d, tf), lambda i, j: (index, 0, j)),
            pl.BlockSpec((None, tf, d), lambda i, j: (index, j, 0)),
            pl.BlockSpec((1, d), lambda i, j: (0, 0)),
        ],
        out_specs=pl.BlockSpec((tm, d), lambda i, j: (i, 0)),
        out_shape=jax.ShapeDtypeStruct((m, d), x.dtype),
        scratch_shapes=[pltpu.VMEM((tm, d), _BF16)],
        compiler_params=pltpu.CompilerParams(
            dimension_semantics=("parallel", "arbitrary"),
            vmem_limit_bytes=_vmem_limit(pipelined, single, temps)),
        name="ffn_final" if final_norm else "ffn",
    )(x, g.reshape(1, d), w_gate, w_up, w_down, final_g.reshape(1, d))


def kernel(x, mix_norm, ab_w_in, ab_rel_bias, ab_conv_w, ab_w_out, c_w_in, c_ln_g, c_ln_b,
           c_w_s, c_b_s, c_w_out, ffn_norm, ffn_w_gate, ffn_w_up, ffn_w_down, final_norm):
    batch, seq, d = x.shape
    depth = mix_norm.shape[0]
    heads = ab_rel_bias.shape[1]

    xf = x.reshape(batch * seq, d)
    w_in = ab_w_in[0].astype(_BF16)
    bias = _bias_table(ab_rel_bias.reshape(-1, ab_rel_bias.shape[2]))
    for layer in range(depth):
        i = layer // 2
        nxt = layer + 1
        next_w = None
        if nxt < depth:
            next_w = (c_w_in if nxt % 2 else ab_w_in, nxt // 2)
        if layer % 2 == 0:
            proj, w_in = _norm_proj(xf, mix_norm[layer], w_in, gelu=False, name="norm_proj",
                                    next_w=next_w)
            attn = _attention(proj, bias, i * heads, batch=batch, seq=seq, heads=heads)
            xf = _conv_out_proj(xf, attn, proj, ab_conv_w[i], ab_w_out, i, seq=seq)
        else:
            z, w_in = _norm_proj(xf, mix_norm[layer], w_in, gelu=True, name="norm_proj_gelu",
                                 next_w=next_w)
            xf = _sgu_out_proj(xf, z, c_ln_g[i], c_ln_b[i], c_w_s[i], c_b_s[i], c_w_out, i)
        xf = _ffn(xf, ffn_norm[layer], ffn_w_gate, ffn_w_up, ffn_w_down, layer, final_norm,
                  final_norm=(layer == depth - 1))
    return xf.reshape(batch, seq, d)
```

```python
import functools

import jax
import jax.numpy as jnp
import numpy as np
from jax import lax
from jax.experimental import pallas as pl
from jax.experimental.pallas import tpu as pltpu

CHUNK = 64
A_HEAD_DIM = 128
A_LEFT_CHUNKS = 8
A_MAX_REL = 256
CONV_WIDTH = 3
C_BLOCK = 128
C_GROUPS = 8
EPS = 1e-6
NEG_INF = -1e30

V7X_VMEM_BYTES = 64 * 1024 * 1024
V7X_BF16_SUBLANES = 16

ROW_TILE_PROJ = 1024
ROW_TILE_MIX = 512
COL_TILE_PROJ = 2048
COL_TILE_FFN = 512
Q_BLOCK = 2 * CHUNK
K_WINDOW = Q_BLOCK + A_LEFT_CHUNKS * CHUNK
ATTN_INTERLEAVE = 32
ATTN_HEADS = 4

_SQRT_HALF = np.sqrt(0.5).astype(np.float32)
_BF16 = jnp.bfloat16


def _vmem_limit(pipelined_bytes, single_bytes, temp_bytes):
    need = 2 * pipelined_bytes + single_bytes + temp_bytes
    return int(min(need + need // 8, V7X_VMEM_BYTES - 4 * 1024 * 1024))


def _nbytes(shape, dtype):
    return int(np.prod(shape)) * jnp.dtype(dtype).itemsize


def _rms_norm(x, g):
    ms = jnp.mean(x * x, axis=-1, keepdims=True)
    return (x * lax.rsqrt(ms + EPS)) * g


def _norm_proj_kernel(*refs, gelu, has_next_w):
    if has_next_w:
        x_ref, g_ref, w_ref, next_w_ref, o_ref, next_w16_ref, h_ref = refs
    else:
        x_ref, g_ref, w_ref, o_ref, h_ref = refs

    def project(h):
        if has_next_w:
            next_w16_ref[...] = next_w_ref[...].astype(next_w16_ref.dtype)
        y = jnp.dot(h, w_ref[...], preferred_element_type=jnp.float32)
        if gelu:
            y = 0.5 * y * (1.0 + lax.erf(y * _SQRT_HALF))
        o_ref[...] = y.astype(o_ref.dtype)

    j = pl.program_id(1)

    @pl.when(j == 0)
    def _():
        h = _rms_norm(x_ref[...], g_ref[...]).astype(h_ref.dtype)
        h_ref[...] = h
        project(h)

    @pl.when(j > 0)
    def _():
        project(h_ref[...])


def _norm_proj(x, g, w, *, gelu, name, next_w=None):
    m, d = x.shape
    n = w.shape[1]
    tm, tn = ROW_TILE_PROJ, COL_TILE_PROJ
    assert m % tm == 0 and n % tn == 0
    grid = (m // tm, n // tn)
    in_specs = [
        pl.BlockSpec((tm, d), lambda i, j: (i, 0)),
        pl.BlockSpec((1, d), lambda i, j: (0, 0)),
        pl.BlockSpec((d, tn), lambda i, j: (0, j)),
    ]
    out_specs = [pl.BlockSpec((tm, tn), lambda i, j: (i, j))]
    out_shape = [jax.ShapeDtypeStruct((m, n), _BF16)]
    operands = [x, g.reshape(1, d), w]
    cast_bytes = 0
    if next_w is not None:
        stacked, index = next_w
        _, r, c = stacked.shape
        n_steps = grid[0] * grid[1]
        n_blocks = max(k for k in range(1, n_steps + 1) if r % (k * V7X_BF16_SUBLANES) == 0)
        rows = r // n_blocks

        def block_of(i, j):
            return jnp.minimum(i * grid[1] + j, n_blocks - 1)

        in_specs.append(pl.BlockSpec((None, rows, c), lambda i, j: (index, block_of(i, j), 0)))
        out_specs.append(pl.BlockSpec((rows, c), lambda i, j: (block_of(i, j), 0)))
        out_shape.append(jax.ShapeDtypeStruct((r, c), _BF16))
        operands.append(stacked)
        cast_bytes = _nbytes((rows, c), stacked.dtype) + _nbytes((rows, c), _BF16)
    pipelined = (_nbytes((tm, d), x.dtype) + _nbytes((d, tn), w.dtype)
                 + _nbytes((tm, tn), _BF16) + cast_bytes)
    scratch = _nbytes((tm, d), _BF16)
    temps = _nbytes((tm, tn), jnp.float32)
    outs = pl.pallas_call(
        functools.partial(_norm_proj_kernel, gelu=gelu, has_next_w=next_w is not None),
        grid=grid,
        in_specs=in_specs,
        out_specs=out_specs,
        out_shape=out_shape,
        scratch_shapes=[pltpu.VMEM((tm, d), _BF16)],
        compiler_params=pltpu.CompilerParams(
            dimension_semantics=("arbitrary", "arbitrary"),
            vmem_limit_bytes=_vmem_limit(pipelined, scratch, temps)),
        name=name,
    )(*operands)
    return outs[0], (outs[1] if next_w is not None else None)


def _bias_table_kernel(g_ref, w_ref, o_ref, w16_ref):
    w16_ref[...] = w_ref[...].astype(w16_ref.dtype)
    width = Q_BLOCK + K_WINDOW
    x = jnp.broadcast_to(g_ref[0], (Q_BLOCK, width))
    row = lax.broadcasted_iota(jnp.int32, (Q_BLOCK, width), 0)
    shift = 1
    while shift < Q_BLOCK:
        x = jnp.where((row & shift) != 0, pltpu.roll(x, shift, 1), x)
        shift *= 2
    t = x[:, Q_BLOCK:]
    q_chunk = lax.broadcasted_iota(jnp.int32, (Q_BLOCK, K_WINDOW), 0) // CHUNK
    k_chunk = lax.broadcasted_iota(jnp.int32, (Q_BLOCK, K_WINDOW), 1) // CHUNK
    band = k_chunk - q_chunk
    o_ref[0] = jnp.where((band >= 0) & (band <= A_LEFT_CHUNKS), t, NEG_INF)


def _bias_table(rel_bias, first_w, index):
    heads, n_rel = rel_bias.shape
    width = Q_BLOCK + K_WINDOW
    _, r, c = first_w.shape
    assert r % (heads * V7X_BF16_SUBLANES) == 0
    rows = r // heads
    left = Q_BLOCK + A_LEFT_CHUNKS * CHUNK - A_MAX_REL
    right = max(0, width - left - n_rel)
    g = jnp.pad(rel_bias[:, ::-1], ((0, 0), (left, right)), mode="edge")[:, :width]
    pipelined = (_nbytes((Q_BLOCK, K_WINDOW), jnp.float32)
                 + _nbytes((rows, c), first_w.dtype) + _nbytes((rows, c), _BF16))
    return pl.pallas_call(
        _bias_table_kernel,
        grid=(heads,),
        in_specs=[pl.BlockSpec((1, 1, width), lambda h: (h, 0, 0)),
                  pl.BlockSpec((None, rows, c), lambda h: (index, h, 0))],
        out_specs=[pl.BlockSpec((1, Q_BLOCK, K_WINDOW), lambda h: (h, 0, 0)),
                   pl.BlockSpec((rows, c), lambda h: (h, 0))],
        out_shape=[jax.ShapeDtypeStruct((heads, Q_BLOCK, K_WINDOW), jnp.float32),
                   jax.ShapeDtypeStruct((r, c), _BF16)],
        compiler_params=pltpu.CompilerParams(
            dimension_semantics=("parallel",),
            vmem_limit_bytes=_vmem_limit(
                pipelined, 0, 6 * _nbytes((Q_BLOCK, width), jnp.float32))),
        name="bias_table",
    )(g.reshape(heads, 1, width), first_w)


def _attention_kernel(q_ref, k_ref, v_ref, b_ref, o_ref, kt_ref, v1_ref):
    seq = q_ref.shape[0]
    dh = A_HEAD_DIM
    kt_ref[...] = k_ref[...].T
    blocks = []
    for hd in range(ATTN_HEADS):
        cols = slice(hd * dh, (hd + 1) * dh)
        v1_ref[hd, :, :dh] = v_ref[:, cols]
        v1_ref[hd, :, dh:] = jnp.ones((seq, dh), v1_ref.dtype)
        for q0 in range(0, seq, Q_BLOCK):
            blocks.append((hd, cols, q0, max(0, q0 + Q_BLOCK - K_WINDOW), q0 + Q_BLOCK))
    for g0 in range(0, len(blocks), ATTN_INTERLEAVE):
        group = blocks[g0:g0 + ATTN_INTERLEAVE]
        s = [jnp.dot(q_ref[q0:k1, cols], kt_ref[cols, k0:k1],
                     preferred_element_type=jnp.float32) for hd, cols, q0, k0, k1 in group]
        s = [si * (dh ** -0.5) + b_ref[hd, :, K_WINDOW - (k1 - k0):]
             for si, (hd, cols, q0, k0, k1) in zip(s, group)]
        m = [jnp.max(si, axis=-1, keepdims=True) for si in s]
        p = [jnp.exp(si - mi).astype(v1_ref.dtype) for si, mi in zip(s, m)]
        o = [jnp.dot(pi, v1_ref[hd, k0:k1, :], preferred_element_type=jnp.float32)
             for pi, (hd, cols, q0, k0, k1) in zip(p, group)]
        for oi, (hd, cols, q0, k0, k1) in zip(o, group):
            o_ref[q0:k1, cols] = (oi[:, :dh] * (1.0 / oi[:, dh:dh + 1])).astype(o_ref.dtype)


def _attention(proj, bias, table0, *, batch, seq, heads):
    dh = A_HEAD_DIM
    hs = ATTN_HEADS
    assert seq % Q_BLOCK == 0 and K_WINDOW % Q_BLOCK == 0
    assert (hs * seq // Q_BLOCK) % ATTN_INTERLEAVE == 0 and heads % hs == 0 and table0 % hs == 0
    n_hb = heads // hs
    pipelined = (4 * _nbytes((seq, hs * dh), proj.dtype)
                 + _nbytes((hs, Q_BLOCK, K_WINDOW), jnp.float32))
    scratch = 3 * _nbytes((seq, hs * dh), proj.dtype)
    temps = 2 * ATTN_INTERLEAVE * _nbytes((Q_BLOCK, K_WINDOW), jnp.float32)
    return pl.pallas_call(
        _attention_kernel,
        grid=(batch, n_hb),
        in_specs=[
            pl.BlockSpec((seq, hs * dh), lambda b, h: (b, h)),
            pl.BlockSpec((seq, hs * dh), lambda b, h: (b, n_hb + h)),
            pl.BlockSpec((seq, hs * dh), lambda b, h: (b, 2 * n_hb + h)),
            pl.BlockSpec((hs, Q_BLOCK, K_WINDOW), lambda b, h: (table0 // hs + h, 0, 0)),
        ],
        out_specs=pl.BlockSpec((seq, hs * dh), lambda b, h: (b, h)),
        out_shape=jax.ShapeDtypeStruct((batch * seq, heads * dh), proj.dtype),
        scratch_shapes=[pltpu.VMEM((hs * dh, seq), proj.dtype),
                        pltpu.VMEM((hs, seq, 2 * dh), proj.dtype)],
        compiler_params=pltpu.CompilerParams(
            dimension_semantics=("parallel", "parallel"),
            vmem_limit_bytes=_vmem_limit(pipelined, scratch, temps)),
        name="band_attention",
    )(proj, proj, proj, bias)


def _conv_out_proj_kernel(x_ref, a_ref, bg_ref, cg_ref, hv_ref, cprev_ref, hprev_ref,
                          cw_ref, w_ref, o_ref, z_ref, lhs_ref, *, rows_per_seq):
    tm, aw = a_ref.shape
    halo = cprev_ref.shape[0]
    f32 = jnp.float32
    i = pl.program_id(0)
    at_seq_start = (i * tm) % rows_per_seq == 0
    zprev = cprev_ref[...].astype(f32) * hprev_ref[...].astype(f32)
    z_ref[:halo] = jnp.where(at_seq_start, 0.0, zprev)
    z_ref[halo:] = cg_ref[...].astype(f32) * hv_ref[...].astype(f32)
    y = cw_ref[0:1] * z_ref[pl.ds(halo - 2, tm)]
    y = y + cw_ref[1:2] * z_ref[pl.ds(halo - 1, tm)]
    y = y + cw_ref[2:3] * z_ref[pl.ds(halo, tm)]
    lhs_ref[:, :aw] = a_ref[...]
    lhs_ref[:, aw:] = (bg_ref[...].astype(f32) * y).astype(lhs_ref.dtype)
    o_ref[...] = x_ref[...] + jnp.dot(lhs_ref[...], w_ref[...].astype(lhs_ref.dtype),
                                      preferred_element_type=jnp.float32)


def _conv_out_proj(x, attn, proj, conv_w, w_out, index, *, seq):
    m, d = x.shape
    aw = attn.shape[1]
    bw = conv_w.shape[1]
    tm = ROW_TILE_MIX
    halo = V7X_BF16_SUBLANES
    assert m % tm == 0 and seq % tm == 0 and CONV_WIDTH - 1 <= halo
    assert conv_w.shape == (CONV_WIDTH, bw) and CONV_WIDTH == 3
    gate0 = (proj.shape[1] - 3 * bw) // bw
    assert gate0 * bw + 3 * bw == proj.shape[1]
    halo_blocks = tm // halo
    w_shape = w_out.shape[1:]

    def prev_rows(col):
        return pl.BlockSpec(
            (halo, bw), lambda i: (jnp.maximum(i * halo_blocks - 1, 0), col))

    pipelined = (2 * _nbytes((tm, d), x.dtype) + _nbytes((tm, aw), attn.dtype)
                 + 3 * _nbytes((tm, bw), proj.dtype))
    single = (_nbytes(w_shape, w_out.dtype) + _nbytes((tm + halo, bw), jnp.float32)
              + _nbytes((tm, aw + bw), _BF16))
    temps = _nbytes(w_shape, _BF16) + 2 * _nbytes((tm, bw), jnp.float32)
    return pl.pallas_call(
        functools.partial(_conv_out_proj_kernel, rows_per_seq=seq),
        grid=(m // tm,),
        in_specs=[
            pl.BlockSpec((tm, d), lambda i: (i, 0)),
            pl.BlockSpec((tm, aw), lambda i: (i, 0)),
            pl.BlockSpec((tm, bw), lambda i: (i, gate0)),
            pl.BlockSpec((tm, bw), lambda i: (i, gate0 + 1)),
            pl.BlockSpec((tm, bw), lambda i: (i, gate0 + 2)),
            prev_rows(gate0 + 1),
            prev_rows(gate0 + 2),
            pl.BlockSpec((CONV_WIDTH, bw), lambda i: (0, 0)),
            pl.BlockSpec((None,) + w_shape, lambda i: (index, 0, 0),
                         pipeline_mode=pl.Buffered(1)),
        ],
        out_specs=pl.BlockSpec((tm, d), lambda i: (i, 0)),
        out_shape=jax.ShapeDtypeStruct((m, d), x.dtype),
        scratch_shapes=[pltpu.VMEM((tm + halo, bw), jnp.float32),
                        pltpu.VMEM((tm, aw + bw), _BF16)],
        compiler_params=pltpu.CompilerParams(
            dimension_semantics=("parallel",),
            vmem_limit_bytes=_vmem_limit(pipelined, single, temps)),
        name="conv_out_proj",
    )(x, attn, proj, proj, proj, proj, proj, conv_w, w_out)


def _sgu_out_proj_kernel(x_ref, u_ref, v_ref, lg_ref, lb_ref, ws_ref, bs_ref, w_ref,
                         o_ref, vn_ref, lhs_ref):
    tm, cw = u_ref.shape
    gw = cw // C_GROUPS
    t_chunk = lax.broadcasted_iota(jnp.int32, (C_BLOCK, C_BLOCK), 0) // CHUNK
    s_chunk = lax.broadcasted_iota(jnp.int32, (C_BLOCK, C_BLOCK), 1) // CHUNK
    causal = s_chunk <= t_chunk
    w_m = [jnp.where(causal, ws_ref[g], 0.0).astype(vn_ref.dtype) for g in range(C_GROUPS)]
    w_out = w_ref[...].astype(lhs_ref.dtype)

    half = tm // 2
    for r0 in range(0, tm, half):
        half_rows = slice(r0, r0 + half)
        v = v_ref[half_rows, :].astype(jnp.float32)
        mu = jnp.mean(v, axis=-1, keepdims=True)
        vc = v - mu
        var = jnp.mean(vc * vc, axis=-1, keepdims=True)
        vn_ref[half_rows, :] = (vc * lax.rsqrt(var + EPS) * lg_ref[...] + lb_ref[...]
                                ).astype(vn_ref.dtype)
        for g in range(C_GROUPS):
            bias = bs_ref[:, g:g + 1]
            cols = slice(g * gw, (g + 1) * gw)
            for n0 in range(r0, r0 + half, C_BLOCK):
                rows = slice(n0, n0 + C_BLOCK)
                s = jnp.dot(w_m[g], vn_ref[rows, cols],
                            preferred_element_type=jnp.float32) + bias
                lhs_ref[rows, cols] = (u_ref[rows, cols].astype(jnp.float32) * s
                                       ).astype(lhs_ref.dtype)
        o_ref[half_rows, :] = x_ref[half_rows, :] + jnp.dot(
            lhs_ref[half_rows, :], w_out, preferred_element_type=jnp.float32)


def _sgu_out_proj(x, z, ln_g, ln_b, w_s, b_s, w_out, index):
    m, d = x.shape
    cw = z.shape[1] // 2
    tm = ROW_TILE_MIX
    assert m % tm == 0 and tm % (2 * C_BLOCK) == 0 and cw % C_GROUPS == 0
    w_shape = w_out.shape[1:]
    pipelined = (2 * _nbytes((tm, d), x.dtype) + 2 * _nbytes((tm, cw), z.dtype)
                 + _nbytes(w_s.shape, w_s.dtype))
    single = _nbytes(w_shape, w_out.dtype) + 2 * _nbytes((tm, cw), _BF16)
    temps = _nbytes(w_shape, _BF16) + 2 * _nbytes((tm // 2, cw), jnp.float32)
    return pl.pallas_call(
        _sgu_out_proj_kernel,
        grid=(m // tm,),
        in_specs=[
            pl.BlockSpec((tm, d), lambda i: (i, 0)),
            pl.BlockSpec((tm, cw), lambda i: (i, 0)),
            pl.BlockSpec((tm, cw), lambda i: (i, 1)),
            pl.BlockSpec((1, cw), lambda i: (0, 0)),
            pl.BlockSpec((1, cw), lambda i: (0, 0)),
            pl.BlockSpec(w_s.shape, lambda i: (0, 0, 0)),
            pl.BlockSpec((C_BLOCK, C_GROUPS), lambda i: (0, 0)),
            pl.BlockSpec((None,) + w_shape, lambda i: (index, 0, 0),
                         pipeline_mode=pl.Buffered(1)),
        ],
        out_specs=pl.BlockSpec((tm, d), lambda i: (i, 0)),
        out_shape=jax.ShapeDtypeStruct((m, d), x.dtype),
        scratch_shapes=[pltpu.VMEM((tm, cw), _BF16), pltpu.VMEM((tm, cw), _BF16)],
        compiler_params=pltpu.CompilerParams(
            dimension_semantics=("parallel",),
            vmem_limit_bytes=_vmem_limit(pipelined, single, temps)),
        name="sgu_out_proj",
    )(x, z, z, ln_g.reshape(1, cw), ln_b.reshape(1, cw), w_s, jnp.transpose(b_s), w_out)


def _ffn_kernel(x_ref, g_ref, wg_ref, wu_ref, wd_ref, fg_ref, o_ref, h_ref, *, final_norm):
    j = pl.program_id(1)
    tf = wg_ref.shape[1]

    def swiglu_tile(h):
        acts = []
        for c0 in range(0, tf, tf // 2):
            cols = slice(c0, c0 + tf // 2)
            gate = jnp.dot(h, wg_ref[:, cols].astype(h.dtype),
                           preferred_element_type=jnp.float32)
            up = jnp.dot(h, wu_ref[:, cols].astype(h.dtype),
                         preferred_element_type=jnp.float32)
            acts.append((gate * jax.nn.sigmoid(gate) * up).astype(h.dtype))
        act = jnp.concatenate(acts, axis=1)
        return jnp.dot(act, wd_ref[...].astype(h.dtype), preferred_element_type=jnp.float32)

    @pl.when(j == 0)
    def _():
        x = x_ref[...]
        h = _rms_norm(x, g_ref[...]).astype(h_ref.dtype)
        h_ref[...] = h
        o_ref[...] = x + swiglu_tile(h)

    @pl.when(j > 0)
    def _():
        o_ref[...] += swiglu_tile(h_ref[...])

    if final_norm:
        @pl.when(j == pl.num_programs(1) - 1)
        def _():
            o_ref[...] = _rms_norm(o_ref[...], fg_ref[...])


def _ffn(x, g, w_gate, w_up, w_down, index, final_g, *, final_norm):
    m, d = x.shape
    f = w_gate.shape[2]
    tm, tf = ROW_TILE_PROJ, COL_TILE_FFN
    assert m % tm == 0 and f % tf == 0
    pipelined = _nbytes((tm, d), x.dtype) + 3 * _nbytes((d, tf), w_gate.dtype)
    single = _nbytes((tm, d), x.dtype) + _nbytes((tm, d), _BF16)
    temps = 2 * _nbytes((d, tf), _BF16) + 2 * _nbytes((tm, tf), jnp.float32)
    return pl.pallas_call(
        functools.partial(_ffn_kernel, final_norm=final_norm),
        grid=(m // tm, f // tf),
        in_specs=[
            pl.BlockSpec((tm, d), lambda i, j: (i, 0), pipeline_mode=pl.Buffered(1)),
            pl.BlockSpec((1, d), lambda i, j: (0, 0)),
            pl.BlockSpec((None, d, tf), lambda i, j: (index, 0, j)),
            pl.BlockSpec((None, d, tf), lambda i, j: (index, 0, j)),
            pl.BlockSpec((None, tf, d), lambda i, j: (index, j, 0)),
            pl.BlockSpec((1, d), lambda i, j: (0, 0)),
        ],
        out_specs=pl.BlockSpec((tm, d), lambda i, j: (i, 0)),
        out_shape=jax.ShapeDtypeStruct((m, d), x.dtype),
        scratch_shapes=[pltpu.VMEM((tm, d), _BF16)],
        compiler_params=pltpu.CompilerParams(
            dimension_semantics=("parallel", "arbitrary"),
            vmem_limit_bytes=_vmem_limit(pipelined, single, temps)),
        name="ffn_final" if final_norm else "ffn",
    )(x, g.reshape(1, d), w_gate, w_up, w_down, final_g.reshape(1, d))


def kernel(x, mix_norm, ab_w_in, ab_rel_bias, ab_conv_w, ab_w_out, c_w_in, c_ln_g, c_ln_b,
           c_w_s, c_b_s, c_w_out, ffn_norm, ffn_w_gate, ffn_w_up, ffn_w_down, final_norm):
    batch, seq, d = x.shape
    depth = mix_norm.shape[0]
    heads = ab_rel_bias.shape[1]

    xf = x.reshape(batch * seq, d)
    bias, w_in = _bias_table(ab_rel_bias.reshape(-1, ab_rel_bias.shape[2]), ab_w_in, 0)
    for layer in range(depth):
        i = layer // 2
        nxt = layer + 1
        next_w = None
        if nxt < depth:
            next_w = (c_w_in if nxt % 2 else ab_w_in, nxt // 2)
        if layer % 2 == 0:
            proj, w_in = _norm_proj(xf, mix_norm[layer], w_in, gelu=False, name="norm_proj",
                                    next_w=next_w)
            attn = _attention(proj, bias, i * heads, batch=batch, seq=seq, heads=heads)
            xf = _conv_out_proj(xf, attn, proj, ab_conv_w[i], ab_w_out, i, seq=seq)
        else:
            z, w_in = _norm_proj(xf, mix_norm[layer], w_in, gelu=True, name="norm_proj_gelu",
                                 next_w=next_w)
            xf = _sgu_out_proj(xf, z, c_ln_g[i], c_ln_b[i], c_w_s[i], c_b_s[i], c_w_out, i)
        xf = _ffn(xf, ffn_norm[layer], ffn_w_gate, ffn_w_up, ffn_w_down, layer, final_norm,
                  final_norm=(layer == depth - 1))
    return xf.reshape(batch, seq, d)
```

```python
import functools

import jax
import jax.numpy as jnp
import numpy as np
from jax import lax
from jax.experimental import pallas as pl
from jax.experimental.pallas import tpu as pltpu

CHUNK = 64
A_HEAD_DIM = 128
A_LEFT_CHUNKS = 8
A_MAX_REL = 256
CONV_WIDTH = 3
C_BLOCK = 128
C_GROUPS = 8
EPS = 1e-6
NEG_INF = -1e30

V7X_VMEM_BYTES = 64 * 1024 * 1024
V7X_BF16_SUBLANES = 16

ROW_TILE_PROJ = 1024
ROW_TILE_MIX = 512
COL_TILE_PROJ = 2048
COL_TILE_FFN = 512
Q_BLOCK = 2 * CHUNK
K_WINDOW = Q_BLOCK + A_LEFT_CHUNKS * CHUNK
ATTN_INTERLEAVE = 32
ATTN_HEADS = 4

_SQRT_HALF = np.sqrt(0.5).astype(np.float32)
_BF16 = jnp.bfloat16


def _vmem_limit(pipelined_bytes, single_bytes, temp_bytes):
    need = 2 * pipelined_bytes + single_bytes + temp_bytes
    return int(min(need + need // 8, V7X_VMEM_BYTES - 4 * 1024 * 1024))


def _nbytes(shape, dtype):
    return int(np.prod(shape)) * jnp.dtype(dtype).itemsize


def _rms_norm(x, g):
    ms = jnp.mean(x * x, axis=-1, keepdims=True)
    return (x * lax.rsqrt(ms + EPS)) * g


def _norm_proj_kernel(*refs, gelu, has_next_w):
    if has_next_w:
        x_ref, g_ref, w_ref, next_w_ref, o_ref, next_w16_ref, h_ref = refs
    else:
        x_ref, g_ref, w_ref, o_ref, h_ref = refs

    def project(h):
        if has_next_w:
            next_w16_ref[...] = next_w_ref[...].astype(next_w16_ref.dtype)
        y = jnp.dot(h, w_ref[...], preferred_element_type=jnp.float32)
        if gelu:
            y = 0.5 * y * (1.0 + lax.erf(y * _SQRT_HALF))
        o_ref[...] = y.astype(o_ref.dtype)

    j = pl.program_id(1)

    @pl.when(j == 0)
    def _():
        h = _rms_norm(x_ref[...], g_ref[...]).astype(h_ref.dtype)
        h_ref[...] = h
        project(h)

    @pl.when(j > 0)
    def _():
        project(h_ref[...])


def _norm_proj(x, g, w, *, gelu, name, next_w=None):
    m, d = x.shape
    n = w.shape[1]
    tm, tn = ROW_TILE_PROJ, COL_TILE_PROJ
    assert m % tm == 0 and n % tn == 0
    grid = (m // tm, n // tn)
    in_specs = [
        pl.BlockSpec((tm, d), lambda i, j: (i, 0)),
        pl.BlockSpec((1, d), lambda i, j: (0, 0)),
        pl.BlockSpec((d, tn), lambda i, j: (0, j)),
    ]
    out_specs = [pl.BlockSpec((tm, tn), lambda i, j: (i, j))]
    out_shape = [jax.ShapeDtypeStruct((m, n), _BF16)]
    operands = [x, g.reshape(1, d), w]
    cast_bytes = 0
    if next_w is not None:
        stacked, index = next_w
        _, r, c = stacked.shape
        n_steps = grid[0] * grid[1]
        n_blocks = max(k for k in range(1, n_steps + 1) if r % (k * V7X_BF16_SUBLANES) == 0)
        rows = r // n_blocks

        def block_of(i, j):
            return jnp.minimum(i * grid[1] + j, n_blocks - 1)

        in_specs.append(pl.BlockSpec((None, rows, c), lambda i, j: (index, block_of(i, j), 0)))
        out_specs.append(pl.BlockSpec((rows, c), lambda i, j: (block_of(i, j), 0)))
        out_shape.append(jax.ShapeDtypeStruct((r, c), _BF16))
        operands.append(stacked)
        cast_bytes = _nbytes((rows, c), stacked.dtype) + _nbytes((rows, c), _BF16)
    pipelined = (_nbytes((tm, d), x.dtype) + _nbytes((d, tn), w.dtype)
                 + _nbytes((tm, tn), _BF16) + cast_bytes)
    scratch = _nbytes((tm, d), _BF16)
    temps = _nbytes((tm, tn), jnp.float32)
    outs = pl.pallas_call(
        functools.partial(_norm_proj_kernel, gelu=gelu, has_next_w=next_w is not None),
        grid=grid,
        in_specs=in_specs,
        out_specs=out_specs,
        out_shape=out_shape,
        scratch_shapes=[pltpu.VMEM((tm, d), _BF16)],
        compiler_params=pltpu.CompilerParams(
            dimension_semantics=("arbitrary", "arbitrary"),
            vmem_limit_bytes=_vmem_limit(pipelined, scratch, temps)),
        name=name,
    )(*operands)
    return outs[0], (outs[1] if next_w is not None else None)


def _bias_table_kernel(g_ref, w_ref, o_ref, w16_ref):
    w16_ref[...] = w_ref[...].astype(w16_ref.dtype)
    width = Q_BLOCK + K_WINDOW
    x = jnp.broadcast_to(g_ref[0], (Q_BLOCK, width))
    row = lax.broadcasted_iota(jnp.int32, (Q_BLOCK, width), 0)
    shift = 1
    while shift < Q_BLOCK:
        x = jnp.where((row & shift) != 0, pltpu.roll(x, shift, 1), x)
        shift *= 2
    t = x[:, Q_BLOCK:]
    q_chunk = lax.broadcasted_iota(jnp.int32, (Q_BLOCK, K_WINDOW), 0) // CHUNK
    k_chunk = lax.broadcasted_iota(jnp.int32, (Q_BLOCK, K_WINDOW), 1) // CHUNK
    band = k_chunk - q_chunk
    o_ref[0] = jnp.where((band >= 0) & (band <= A_LEFT_CHUNKS), t, NEG_INF)


def _bias_table(rel_bias, first_w, index):
    heads, n_rel = rel_bias.shape
    width = Q_BLOCK + K_WINDOW
    _, r, c = first_w.shape
    assert r % (heads * V7X_BF16_SUBLANES) == 0
    rows = r // heads
    left = Q_BLOCK + A_LEFT_CHUNKS * CHUNK - A_MAX_REL
    right = max(0, width - left - n_rel)
    g = jnp.pad(rel_bias[:, ::-1], ((0, 0), (left, right)), mode="edge")[:, :width]
    pipelined = (_nbytes((Q_BLOCK, K_WINDOW), jnp.float32)
                 + _nbytes((rows, c), first_w.dtype) + _nbytes((rows, c), _BF16))
    return pl.pallas_call(
        _bias_table_kernel,
        grid=(heads,),
        in_specs=[pl.BlockSpec((1, 1, width), lambda h: (h, 0, 0)),
                  pl.BlockSpec((None, rows, c), lambda h: (index, h, 0))],
        out_specs=[pl.BlockSpec((1, Q_BLOCK, K_WINDOW), lambda h: (h, 0, 0)),
                   pl.BlockSpec((rows, c), lambda h: (h, 0))],
        out_shape=[jax.ShapeDtypeStruct((heads, Q_BLOCK, K_WINDOW), jnp.float32),
                   jax.ShapeDtypeStruct((r, c), _BF16)],
        compiler_params=pltpu.CompilerParams(
            dimension_semantics=("parallel",),
            vmem_limit_bytes=_vmem_limit(
                pipelined, _nbytes((r, c), _BF16),
                6 * _nbytes((Q_BLOCK, width), jnp.float32))),
        name="bias_table",
    )(g.reshape(heads, 1, width), first_w)


def _attention_kernel(q_ref, k_ref, v_ref, b_ref, o_ref, kt_ref, v1_ref):
    seq = q_ref.shape[0]
    dh = A_HEAD_DIM
    kt_ref[...] = k_ref[...].T
    blocks = []
    for hd in range(ATTN_HEADS):
        cols = slice(hd * dh, (hd + 1) * dh)
        v1_ref[hd, :, :dh] = v_ref[:, cols]
        v1_ref[hd, :, dh:] = jnp.ones((seq, dh), v1_ref.dtype)
        for q0 in range(0, seq, Q_BLOCK):
            blocks.append((hd, cols, q0, max(0, q0 + Q_BLOCK - K_WINDOW), q0 + Q_BLOCK))
    for g0 in range(0, len(blocks), ATTN_INTERLEAVE):
        group = blocks[g0:g0 + ATTN_INTERLEAVE]
        s = [jnp.dot(q_ref[q0:k1, cols], kt_ref[cols, k0:k1],
                     preferred_element_type=jnp.float32) for hd, cols, q0, k0, k1 in group]
        s = [si * (dh ** -0.5) + b_ref[hd, :, K_WINDOW - (k1 - k0):]
             for si, (hd, cols, q0, k0, k1) in zip(s, group)]
        m = [jnp.max(si, axis=-1, keepdims=True) for si in s]
        p = [jnp.exp(si - mi).astype(v1_ref.dtype) for si, mi in zip(s, m)]
        o = [jnp.dot(pi, v1_ref[hd, k0:k1, :], preferred_element_type=jnp.float32)
             for pi, (hd, cols, q0, k0, k1) in zip(p, group)]
        for oi, (hd, cols, q0, k0, k1) in zip(o, group):
            o_ref[q0:k1, cols] = (oi[:, :dh] * (1.0 / oi[:, dh:dh + 1])).astype(o_ref.dtype)


def _attention(proj, bias, table0, *, batch, seq, heads):
    dh = A_HEAD_DIM
    hs = ATTN_HEADS
    assert seq % Q_BLOCK == 0 and K_WINDOW % Q_BLOCK == 0
    assert (hs * seq // Q_BLOCK) % ATTN_INTERLEAVE == 0 and heads % hs == 0 and table0 % hs == 0
    n_hb = heads // hs
    pipelined = (4 * _nbytes((seq, hs * dh), proj.dtype)
                 + _nbytes((hs, Q_BLOCK, K_WINDOW), jnp.float32))
    scratch = 3 * _nbytes((seq, hs * dh), proj.dtype)
    temps = 2 * ATTN_INTERLEAVE * _nbytes((Q_BLOCK, K_WINDOW), jnp.float32)
    return pl.pallas_call(
        _attention_kernel,
        grid=(batch, n_hb),
        in_specs=[
            pl.BlockSpec((seq, hs * dh), lambda b, h: (b, h)),
            pl.BlockSpec((seq, hs * dh), lambda b, h: (b, n_hb + h)),
            pl.BlockSpec((seq, hs * dh), lambda b, h: (b, 2 * n_hb + h)),
            pl.BlockSpec((hs, Q_BLOCK, K_WINDOW), lambda b, h: (table0 // hs + h, 0, 0)),
        ],
        out_specs=pl.BlockSpec((seq, hs * dh), lambda b, h: (b, h)),
        out_shape=jax.ShapeDtypeStruct((batch * seq, heads * dh), proj.dtype),
        scratch_shapes=[pltpu.VMEM((hs * dh, seq), proj.dtype),
                        pltpu.VMEM((hs, seq, 2 * dh), proj.dtype)],
        compiler_params=pltpu.CompilerParams(
            dimension_semantics=("parallel", "parallel"),
            vmem_limit_bytes=_vmem_limit(pipelined, scratch, temps)),
        name="band_attention",
    )(proj, proj, proj, bias)


def _conv_out_proj_kernel(x_ref, a_ref, bg_ref, cg_ref, hv_ref, cprev_ref, hprev_ref,
                          cw_ref, w_ref, o_ref, z_ref, w16_ref, *, rows_per_seq):
    tm, aw = a_ref.shape
    halo = cprev_ref.shape[0]
    f32 = jnp.float32
    i = pl.program_id(0)

    @pl.when(i == 0)
    def _():
        w16_ref[...] = w_ref[...].astype(w16_ref.dtype)

    acc = jnp.dot(a_ref[...], w16_ref[:aw, :], preferred_element_type=f32)
    at_seq_start = (i * tm) % rows_per_seq == 0
    zprev = cprev_ref[...].astype(f32) * hprev_ref[...].astype(f32)
    z_ref[:halo] = jnp.where(at_seq_start, 0.0, zprev)
    z_ref[halo:] = cg_ref[...].astype(f32) * hv_ref[...].astype(f32)
    y = cw_ref[0:1] * z_ref[pl.ds(halo - 2, tm)]
    y = y + cw_ref[1:2] * z_ref[pl.ds(halo - 1, tm)]
    y = y + cw_ref[2:3] * z_ref[pl.ds(halo, tm)]
    conv = (bg_ref[...].astype(f32) * y).astype(a_ref.dtype)
    acc = acc + jnp.dot(conv, w16_ref[aw:, :], preferred_element_type=f32)
    o_ref[...] = x_ref[...] + acc


def _conv_out_proj(x, attn, proj, conv_w, w_out, index, *, seq):
    m, d = x.shape
    aw = attn.shape[1]
    bw = conv_w.shape[1]
    tm = ROW_TILE_MIX
    halo = V7X_BF16_SUBLANES
    assert m % tm == 0 and seq % tm == 0 and CONV_WIDTH - 1 <= halo
    assert conv_w.shape == (CONV_WIDTH, bw) and CONV_WIDTH == 3
    gate0 = (proj.shape[1] - 3 * bw) // bw
    assert gate0 * bw + 3 * bw == proj.shape[1]
    halo_blocks = tm // halo
    w_shape = w_out.shape[1:]

    def prev_rows(col):
        return pl.BlockSpec(
            (halo, bw), lambda i: (jnp.maximum(i * halo_blocks - 1, 0), col))

    pipelined = (2 * _nbytes((tm, d), x.dtype) + _nbytes((tm, aw), attn.dtype)
                 + 3 * _nbytes((tm, bw), proj.dtype))
    single = (_nbytes(w_shape, w_out.dtype) + _nbytes(w_shape, _BF16)
              + _nbytes((tm + halo, bw), jnp.float32))
    temps = 2 * _nbytes((tm, bw), jnp.float32) + _nbytes((tm, d), jnp.float32)
    return pl.pallas_call(
        functools.partial(_conv_out_proj_kernel, rows_per_seq=seq),
        grid=(m // tm,),
        in_specs=[
            pl.BlockSpec((tm, d), lambda i: (i, 0)),
            pl.BlockSpec((tm, aw), lambda i: (i, 0)),
            pl.BlockSpec((tm, bw), lambda i: (i, gate0)),
            pl.BlockSpec((tm, bw), lambda i: (i, gate0 + 1)),
            pl.BlockSpec((tm, bw), lambda i: (i, gate0 + 2)),
            prev_rows(gate0 + 1),
            prev_rows(gate0 + 2),
            pl.BlockSpec((CONV_WIDTH, bw), lambda i: (0, 0)),
            pl.BlockSpec((None,) + w_shape, lambda i: (index, 0, 0),
                         pipeline_mode=pl.Buffered(1)),
        ],
        out_specs=pl.BlockSpec((tm, d), lambda i: (i, 0)),
        out_shape=jax.ShapeDtypeStruct((m, d), x.dtype),
        scratch_shapes=[pltpu.VMEM((tm + halo, bw), jnp.float32), pltpu.VMEM(w_shape, _BF16)],
        compiler_params=pltpu.CompilerParams(
            dimension_semantics=("arbitrary",),
            vmem_limit_bytes=_vmem_limit(pipelined, single, temps)),
        name="conv_out_proj",
    )(x, attn, proj, proj, proj, proj, proj, conv_w, w_out)


def _sgu_out_proj_kernel(x_ref, u_ref, v_ref, lg_ref, lb_ref, ws_ref, bs_ref, w_ref,
                         o_ref, vn_ref, lhs_ref, w16_ref):
    tm, cw = u_ref.shape
    gw = cw // C_GROUPS

    @pl.when(pl.program_id(0) == 0)
    def _():
        w16_ref[...] = w_ref[...].astype(w16_ref.dtype)

    t_chunk = lax.broadcasted_iota(jnp.int32, (C_BLOCK, C_BLOCK), 0) // CHUNK
    s_chunk = lax.broadcasted_iota(jnp.int32, (C_BLOCK, C_BLOCK), 1) // CHUNK
    causal = s_chunk <= t_chunk
    w_m = [jnp.where(causal, ws_ref[g], 0.0).astype(vn_ref.dtype) for g in range(C_GROUPS)]

    half = tm // 2
    for r0 in range(0, tm, half):
        half_rows = slice(r0, r0 + half)
        v = v_ref[half_rows, :].astype(jnp.float32)
        mu = jnp.mean(v, axis=-1, keepdims=True)
        vc = v - mu
        var = jnp.mean(vc * vc, axis=-1, keepdims=True)
        vn_ref[half_rows, :] = (vc * lax.rsqrt(var + EPS) * lg_ref[...] + lb_ref[...]
                                ).astype(vn_ref.dtype)
        for g in range(C_GROUPS):
            bias = bs_ref[:, g:g + 1]
            cols = slice(g * gw, (g + 1) * gw)
            for n0 in range(r0, r0 + half, C_BLOCK):
                rows = slice(n0, n0 + C_BLOCK)
                s = jnp.dot(w_m[g], vn_ref[rows, cols],
                            preferred_element_type=jnp.float32) + bias
                lhs_ref[rows, cols] = (u_ref[rows, cols].astype(jnp.float32) * s
                                       ).astype(lhs_ref.dtype)
        o_ref[half_rows, :] = x_ref[half_rows, :] + jnp.dot(
            lhs_ref[half_rows, :], w16_ref[...], preferred_element_type=jnp.float32)


def _sgu_out_proj(x, z, ln_g, ln_b, w_s, b_s, w_out, index):
    m, d = x.shape
    cw = z.shape[1] // 2
    tm = ROW_TILE_MIX
    assert m % tm == 0 and tm % (2 * C_BLOCK) == 0 and cw % C_GROUPS == 0
    w_shape = w_out.shape[1:]
    pipelined = (2 * _nbytes((tm, d), x.dtype) + 2 * _nbytes((tm, cw), z.dtype)
                 + _nbytes(w_s.shape, w_s.dtype))
    single = (_nbytes(w_shape, w_out.dtype) + _nbytes(w_shape, _BF16)
              + 2 * _nbytes((tm, cw), _BF16))
    temps = 2 * _nbytes((tm // 2, cw), jnp.float32)
    return pl.pallas_call(
        _sgu_out_proj_kernel,
        grid=(m // tm,),
        in_specs=[
            pl.BlockSpec((tm, d), lambda i: (i, 0)),
            pl.BlockSpec((tm, cw), lambda i: (i, 0)),
            pl.BlockSpec((tm, cw), lambda i: (i, 1)),
            pl.BlockSpec((1, cw), lambda i: (0, 0)),
            pl.BlockSpec((1, cw), lambda i: (0, 0)),
            pl.BlockSpec(w_s.shape, lambda i: (0, 0, 0)),
            pl.BlockSpec((C_BLOCK, C_GROUPS), lambda i: (0, 0)),
            pl.BlockSpec((None,) + w_shape, lambda i: (index, 0, 0),
                         pipeline_mode=pl.Buffered(1)),
        ],
        out_specs=pl.BlockSpec((tm, d), lambda i: (i, 0)),
        out_shape=jax.ShapeDtypeStruct((m, d), x.dtype),
        scratch_shapes=[pltpu.VMEM((tm, cw), _BF16), pltpu.VMEM((tm, cw), _BF16),
                        pltpu.VMEM(w_shape, _BF16)],
        compiler_params=pltpu.CompilerParams(
            dimension_semantics=("arbitrary",),
            vmem_limit_bytes=_vmem_limit(pipelined, single, temps)),
        name="sgu_out_proj",
    )(x, z, z, ln_g.reshape(1, cw), ln_b.reshape(1, cw), w_s, jnp.transpose(b_s), w_out)


def _ffn_kernel(x_ref, g_ref, wg_ref, wu_ref, wd_ref, fg_ref, o_ref, h_ref, *, final_norm):
    j = pl.program_id(1)
    tf = wg_ref.shape[1]

    def swiglu_tile(h):
        acts = []
        for c0 in range(0, tf, tf // 2):
            cols = slice(c0, c0 + tf // 2)
            gate = jnp.dot(h, wg_ref[:, cols].astype(h.dtype),
                           preferred_element_type=jnp.float32)
            up = jnp.dot(h, wu_ref[:, cols].astype(h.dtype),
                         preferred_element_type=jnp.float32)
            acts.append((gate * jax.nn.sigmoid(gate) * up).astype(h.dtype))
        act = jnp.concatenate(acts, axis=1)
        return jnp.dot(act, wd_ref[...].astype(h.dtype), preferred_element_type=jnp.float32)

    @pl.when(j == 0)
    def _():
        x = x_ref[...]
        h = _rms_norm(x, g_ref[...]).astype(h_ref.dtype)
        h_ref[...] = h
        o_ref[...] = x + swiglu_tile(h)

    @pl.when(j > 0)
    def _():
        o_ref[...] += swiglu_tile(h_ref[...])

    if final_norm:
        @pl.when(j == pl.num_programs(1) - 1)
        def _():
            o_ref[...] = _rms_norm(o_ref[...], fg_ref[...])


def _ffn(x, g, w_gate, w_up, w_down, index, final_g, *, final_norm):
    m, d = x.shape
    f = w_gate.shape[2]
    tm, tf = ROW_TILE_PROJ, COL_TILE_FFN
    assert m % tm == 0 and f % tf == 0
    pipelined = _nbytes((tm, d), x.dtype) + 3 * _nbytes((d, tf), w_gate.dtype)
    single = _nbytes((tm, d), x.dtype) + _nbytes((tm, d), _BF16)
    temps = 2 * _nbytes((d, tf), _BF16) + 2 * _nbytes((tm, tf), jnp.float32)
    return pl.pallas_call(
        functools.partial(_ffn_kernel, final_norm=final_norm),
        grid=(m // tm, f // tf),
        in_specs=[
            pl.BlockSpec((tm, d), lambda i, j: (i, 0), pipeline_mode=pl.Buffered(1)),
            pl.BlockSpec((1, d), lambda i, j: (0, 0)),
            pl.BlockSpec((None, d, tf), lambda i, j: (index, 0, j)),
            pl.BlockSpec((None, d, tf), lambda i, j: (index, 0, j)),
            pl.BlockSpec((None, tf, d), lambda i, j: (index, j, 0)),
            pl.BlockSpec((1, d), lambda i, j: (0, 0)),
        ],
        out_specs=pl.BlockSpec((tm, d), lambda i, j: (i, 0)),
        out_shape=jax.ShapeDtypeStruct((m, d), x.dtype),
        scratch_shapes=[pltpu.VMEM((tm, d), _BF16)],
        compiler_params=pltpu.CompilerParams(
            dimension_semantics=("parallel", "arbitrary"),
            vmem_limit_bytes=_vmem_limit(pipelined, single, temps)),
        name="ffn_final" if final_norm else "ffn",
    )(x, g.reshape(1, d), w_gate, w_up, w_down, final_g.reshape(1, d))


def kernel(x, mix_norm, ab_w_in, ab_rel_bias, ab_conv_w, ab_w_out, c_w_in, c_ln_g, c_ln_b,
           c_w_s, c_b_s, c_w_out, ffn_norm, ffn_w_gate, ffn_w_up, ffn_w_down, final_norm):
    batch, seq, d = x.shape
    depth = mix_norm.shape[0]
    heads = ab_rel_bias.shape[1]

    xf = x.reshape(batch * seq, d)
    bias, w_in = _bias_table(ab_rel_bias.reshape(-1, ab_rel_bias.shape[2]), ab_w_in, 0)
    for layer in range(depth):
        i = layer // 2
        nxt = layer + 1
        next_w = None
        if nxt < depth:
            next_w = (c_w_in if nxt % 2 else ab_w_in, nxt // 2)
        if layer % 2 == 0:
            proj, w_in = _norm_proj(xf, mix_norm[layer], w_in, gelu=False, name="norm_proj",
                                    next_w=next_w)
            attn = _attention(proj, bias, i * heads, batch=batch, seq=seq, heads=heads)
            xf = _conv_out_proj(xf, attn, proj, ab_conv_w[i], ab_w_out, i, seq=seq)
        else:
            z, w_in = _norm_proj(xf, mix_norm[layer], w_in, gelu=True, name="norm_proj_gelu",
                                 next_w=next_w)
            xf = _sgu_out_proj(xf, z, c_ln_g[i], c_ln_b[i], c_w_s[i], c_b_s[i], c_w_out, i)
        xf = _ffn(xf, ffn_norm[layer], ffn_w_gate, ffn_w_up, ffn_w_down, layer, final_norm,
                  final_norm=(layer == depth - 1))
    return xf.reshape(batch, seq, d)
```

```python
import functools

import jax
import jax.numpy as jnp
import numpy as np
from jax import lax
from jax.experimental import pallas as pl
from jax.experimental.pallas import tpu as pltpu

CHUNK = 64
A_HEAD_DIM = 128
A_LEFT_CHUNKS = 8
A_MAX_REL = 256
CONV_WIDTH = 3
C_BLOCK = 128
C_GROUPS = 8
EPS = 1e-6
NEG_INF = -1e30

V7X_VMEM_BYTES = 64 * 1024 * 1024
V7X_BF16_SUBLANES = 16
VMEM_CAP_BYTES = V7X_VMEM_BYTES - 4 * 1024 * 1024
VMEM_SLACK_DIV = 8

ROW_TILE_PROJ = 1024
ROW_TILE_MIX = 512
COL_TILE_PROJ = 2048
COL_TILE_FFN = 512
Q_BLOCK = 2 * CHUNK
K_WINDOW = Q_BLOCK + A_LEFT_CHUNKS * CHUNK
ATTN_INTERLEAVE = 32
ATTN_HEADS = 4

_SQRT_HALF = np.sqrt(0.5).astype(np.float32)
_BF16 = jnp.bfloat16


def _vmem_limit(pipelined_bytes, single_bytes, temp_bytes):
    need = 2 * pipelined_bytes + single_bytes + temp_bytes
    return int(min(need + need // VMEM_SLACK_DIV, VMEM_CAP_BYTES))


def _nbytes(shape, dtype):
    return int(np.prod(shape)) * jnp.dtype(dtype).itemsize


def _rms_norm(x, g):
    ms = jnp.mean(x * x, axis=-1, keepdims=True)
    return (x * lax.rsqrt(ms + EPS)) * g


def _norm_proj_kernel(*refs, gelu, has_next_w):
    if has_next_w:
        x_ref, g_ref, w_ref, next_w_ref, o_ref, next_w16_ref, h_ref = refs
    else:
        x_ref, g_ref, w_ref, o_ref, h_ref = refs

    def project(h):
        if has_next_w:
            next_w16_ref[...] = next_w_ref[...].astype(next_w16_ref.dtype)
        y = jnp.dot(h, w_ref[...], preferred_element_type=jnp.float32)
        if gelu:
            y = 0.5 * y * (1.0 + lax.erf(y * _SQRT_HALF))
        o_ref[...] = y.astype(o_ref.dtype)

    j = pl.program_id(1)

    @pl.when(j == 0)
    def _():
        h = _rms_norm(x_ref[...], g_ref[...]).astype(h_ref.dtype)
        h_ref[...] = h
        project(h)

    @pl.when(j > 0)
    def _():
        project(h_ref[...])


def _norm_proj(x, g, w, *, gelu, name, next_w=None):
    m, d = x.shape
    n = w.shape[1]
    tm, tn = ROW_TILE_PROJ, COL_TILE_PROJ
    assert m % tm == 0 and n % tn == 0
    grid = (m // tm, n // tn)
    in_specs = [
        pl.BlockSpec((tm, d), lambda i, j: (i, 0)),
        pl.BlockSpec((1, d), lambda i, j: (0, 0)),
        pl.BlockSpec((d, tn), lambda i, j: (0, j)),
    ]
    out_specs = [pl.BlockSpec((tm, tn), lambda i, j: (i, j))]
    out_shape = [jax.ShapeDtypeStruct((m, n), _BF16)]
    operands = [x, g.reshape(1, d), w]
    cast_bytes = 0
    if next_w is not None:
        stacked, index = next_w
        _, r, c = stacked.shape
        n_steps = grid[0] * grid[1]
        n_blocks = max(k for k in range(1, n_steps + 1) if r % (k * V7X_BF16_SUBLANES) == 0)
        rows = r // n_blocks

        def block_of(i, j):
            return jnp.minimum(i * grid[1] + j, n_blocks - 1)

        in_specs.append(pl.BlockSpec((None, rows, c), lambda i, j: (index, block_of(i, j), 0)))
        out_specs.append(pl.BlockSpec((rows, c), lambda i, j: (block_of(i, j), 0)))
        out_shape.append(jax.ShapeDtypeStruct((r, c), _BF16))
        operands.append(stacked)
        cast_bytes = _nbytes((rows, c), stacked.dtype) + _nbytes((rows, c), _BF16)
    pipelined = (_nbytes((tm, d), x.dtype) + _nbytes((d, tn), w.dtype)
                 + _nbytes((tm, tn), _BF16) + cast_bytes)
    scratch = _nbytes((tm, d), _BF16)
    temps = _nbytes((tm, tn), jnp.float32)
    outs = pl.pallas_call(
        functools.partial(_norm_proj_kernel, gelu=gelu, has_next_w=next_w is not None),
        grid=grid,
        in_specs=in_specs,
        out_specs=out_specs,
        out_shape=out_shape,
        scratch_shapes=[pltpu.VMEM((tm, d), _BF16)],
        compiler_params=pltpu.CompilerParams(
            dimension_semantics=("arbitrary", "arbitrary"),
            vmem_limit_bytes=_vmem_limit(pipelined, scratch, temps)),
        name=name,
    )(*operands)
    return outs[0], (outs[1] if next_w is not None else None)


def _bias_table_kernel(g_ref, w_ref, o_ref, w16_ref):
    w16_ref[...] = w_ref[...].astype(w16_ref.dtype)
    width = Q_BLOCK + K_WINDOW
    x = jnp.broadcast_to(g_ref[0], (Q_BLOCK, width))
    row = lax.broadcasted_iota(jnp.int32, (Q_BLOCK, width), 0)
    shift = 1
    while shift < Q_BLOCK:
        x = jnp.where((row & shift) != 0, pltpu.roll(x, shift, 1), x)
        shift *= 2
    t = x[:, Q_BLOCK:]
    q_chunk = lax.broadcasted_iota(jnp.int32, (Q_BLOCK, K_WINDOW), 0) // CHUNK
    k_chunk = lax.broadcasted_iota(jnp.int32, (Q_BLOCK, K_WINDOW), 1) // CHUNK
    band = k_chunk - q_chunk
    o_ref[0] = jnp.where((band >= 0) & (band <= A_LEFT_CHUNKS), t, NEG_INF)


def _bias_table(rel_bias, first_w, index):
    heads, n_rel = rel_bias.shape
    width = Q_BLOCK + K_WINDOW
    _, r, c = first_w.shape
    assert r % (heads * V7X_BF16_SUBLANES) == 0
    rows = r // heads
    left = Q_BLOCK + A_LEFT_CHUNKS * CHUNK - A_MAX_REL
    right = max(0, width - left - n_rel)
    g = jnp.pad(rel_bias[:, ::-1], ((0, 0), (left, right)), mode="edge")[:, :width]
    pipelined = (_nbytes((Q_BLOCK, K_WINDOW), jnp.float32)
                 + _nbytes((rows, c), first_w.dtype) + _nbytes((rows, c), _BF16))
    return pl.pallas_call(
        _bias_table_kernel,
        grid=(heads,),
        in_specs=[pl.BlockSpec((1, 1, width), lambda h: (h, 0, 0)),
                  pl.BlockSpec((None, rows, c), lambda h: (index, h, 0))],
        out_specs=[pl.BlockSpec((1, Q_BLOCK, K_WINDOW), lambda h: (h, 0, 0)),
                   pl.BlockSpec((rows, c), lambda h: (h, 0))],
        out_shape=[jax.ShapeDtypeStruct((heads, Q_BLOCK, K_WINDOW), jnp.float32),
                   jax.ShapeDtypeStruct((r, c), _BF16)],
        compiler_params=pltpu.CompilerParams(
            dimension_semantics=("parallel",),
            vmem_limit_bytes=_vmem_limit(
                pipelined, _nbytes((r, c), _BF16),
                6 * _nbytes((Q_BLOCK, width), jnp.float32))),
        name="bias_table",
    )(g.reshape(heads, 1, width), first_w)


def _attention_kernel(q_ref, k_ref, v_ref, b_ref, o_ref, kt_ref, v1_ref):
    seq = q_ref.shape[0]
    dh = A_HEAD_DIM
    kt_ref[...] = k_ref[...].T
    blocks = []
    for hd in range(ATTN_HEADS):
        cols = slice(hd * dh, (hd + 1) * dh)
        v1_ref[hd, :, :dh] = v_ref[:, cols]
        v1_ref[hd, :, dh:] = jnp.ones((seq, dh), v1_ref.dtype)
        for q0 in range(0, seq, Q_BLOCK):
            blocks.append((hd, cols, q0, max(0, q0 + Q_BLOCK - K_WINDOW), q0 + Q_BLOCK))
    for g0 in range(0, len(blocks), ATTN_INTERLEAVE):
        group = blocks[g0:g0 + ATTN_INTERLEAVE]
        s = [jnp.dot(q_ref[q0:k1, cols], kt_ref[cols, k0:k1],
                     preferred_element_type=jnp.float32) for hd, cols, q0, k0, k1 in group]
        s = [si * (dh ** -0.5) + b_ref[hd, :, K_WINDOW - (k1 - k0):]
             for si, (hd, cols, q0, k0, k1) in zip(s, group)]
        m = [jnp.max(si, axis=-1, keepdims=True) for si in s]
        p = [jnp.exp(si - mi).astype(v1_ref.dtype) for si, mi in zip(s, m)]
        o = [jnp.dot(pi, v1_ref[hd, k0:k1, :], preferred_element_type=jnp.float32)
             for pi, (hd, cols, q0, k0, k1) in zip(p, group)]
        for oi, (hd, cols, q0, k0, k1) in zip(o, group):
            o_ref[q0:k1, cols] = (oi[:, :dh] * (1.0 / oi[:, dh:dh + 1])).astype(o_ref.dtype)


def _attention(proj, bias, table0, *, batch, seq, heads):
    dh = A_HEAD_DIM
    hs = ATTN_HEADS
    assert seq % Q_BLOCK == 0 and K_WINDOW % Q_BLOCK == 0
    assert (hs * seq // Q_BLOCK) % ATTN_INTERLEAVE == 0 and heads % hs == 0 and table0 % hs == 0
    n_hb = heads // hs
    pipelined = (4 * _nbytes((seq, hs * dh), proj.dtype)
                 + _nbytes((hs, Q_BLOCK, K_WINDOW), jnp.float32))
    scratch = 3 * _nbytes((seq, hs * dh), proj.dtype)
    temps = 2 * ATTN_INTERLEAVE * _nbytes((Q_BLOCK, K_WINDOW), jnp.float32)
    return pl.pallas_call(
        _attention_kernel,
        grid=(batch, n_hb),
        in_specs=[
            pl.BlockSpec((seq, hs * dh), lambda b, h: (b, h)),
            pl.BlockSpec((seq, hs * dh), lambda b, h: (b, n_hb + h)),
            pl.BlockSpec((seq, hs * dh), lambda b, h: (b, 2 * n_hb + h)),
            pl.BlockSpec((hs, Q_BLOCK, K_WINDOW), lambda b, h: (table0 // hs + h, 0, 0)),
        ],
        out_specs=pl.BlockSpec((seq, hs * dh), lambda b, h: (b, h)),
        out_shape=jax.ShapeDtypeStruct((batch * seq, heads * dh), proj.dtype),
        scratch_shapes=[pltpu.VMEM((hs * dh, seq), proj.dtype),
                        pltpu.VMEM((hs, seq, 2 * dh), proj.dtype)],
        compiler_params=pltpu.CompilerParams(
            dimension_semantics=("parallel", "parallel"),
            vmem_limit_bytes=_vmem_limit(pipelined, scratch, temps)),
        name="band_attention",
    )(proj, proj, proj, bias)


def _conv_out_proj_kernel(x_ref, a_ref, bg_ref, cg_ref, hv_ref, cprev_ref, hprev_ref,
                          cw_ref, w_ref, o_ref, z_ref, lhs_ref, *, rows_per_seq):
    tm, aw = a_ref.shape
    halo = cprev_ref.shape[0]
    f32 = jnp.float32
    i = pl.program_id(0)
    at_seq_start = (i * tm) % rows_per_seq == 0
    zprev = cprev_ref[...].astype(f32) * hprev_ref[...].astype(f32)
    z_ref[:halo] = jnp.where(at_seq_start, 0.0, zprev)
    z_ref[halo:] = cg_ref[...].astype(f32) * hv_ref[...].astype(f32)
    y = cw_ref[0:1] * z_ref[pl.ds(halo - 2, tm)]
    y = y + cw_ref[1:2] * z_ref[pl.ds(halo - 1, tm)]
    y = y + cw_ref[2:3] * z_ref[pl.ds(halo, tm)]
    lhs_ref[:, :aw] = a_ref[...]
    lhs_ref[:, aw:] = (bg_ref[...].astype(f32) * y).astype(lhs_ref.dtype)
    o_ref[...] = x_ref[...] + jnp.dot(lhs_ref[...], w_ref[...].astype(lhs_ref.dtype),
                                      preferred_element_type=f32)


def _conv_out_proj(x, attn, proj, conv_w, w_out, index, *, seq):
    m, d = x.shape
    aw = attn.shape[1]
    bw = conv_w.shape[1]
    tm = ROW_TILE_MIX
    halo = V7X_BF16_SUBLANES
    assert m % tm == 0 and seq % tm == 0 and CONV_WIDTH - 1 <= halo
    assert conv_w.shape == (CONV_WIDTH, bw) and CONV_WIDTH == 3
    gate0 = (proj.shape[1] - 3 * bw) // bw
    assert gate0 * bw + 3 * bw == proj.shape[1]
    halo_blocks = tm // halo
    w_shape = w_out.shape[1:]

    def prev_rows(col):
        return pl.BlockSpec(
            (halo, bw), lambda i: (jnp.maximum(i * halo_blocks - 1, 0), col))

    pipelined = (2 * _nbytes((tm, d), x.dtype) + _nbytes((tm, aw), attn.dtype)
                 + 3 * _nbytes((tm, bw), proj.dtype))
    single = (_nbytes(w_shape, w_out.dtype) + _nbytes((tm + halo, bw), jnp.float32)
              + _nbytes((tm, aw + bw), _BF16))
    temps = _nbytes(w_shape, _BF16) + 2 * _nbytes((tm, bw), jnp.float32)
    return pl.pallas_call(
        functools.partial(_conv_out_proj_kernel, rows_per_seq=seq),
        grid=(m // tm,),
        in_specs=[
            pl.BlockSpec((tm, d), lambda i: (i, 0)),
            pl.BlockSpec((tm, aw), lambda i: (i, 0)),
            pl.BlockSpec((tm, bw), lambda i: (i, gate0)),
            pl.BlockSpec((tm, bw), lambda i: (i, gate0 + 1)),
            pl.BlockSpec((tm, bw), lambda i: (i, gate0 + 2)),
            prev_rows(gate0 + 1),
            prev_rows(gate0 + 2),
            pl.BlockSpec((CONV_WIDTH, bw), lambda i: (0, 0)),
            pl.BlockSpec((None,) + w_shape, lambda i: (index, 0, 0),
                         pipeline_mode=pl.Buffered(1)),
        ],
        out_specs=pl.BlockSpec((tm, d), lambda i: (i, 0)),
        out_shape=jax.ShapeDtypeStruct((m, d), x.dtype),
        scratch_shapes=[pltpu.VMEM((tm + halo, bw), jnp.float32),
                        pltpu.VMEM((tm, aw + bw), _BF16)],
        compiler_params=pltpu.CompilerParams(
            dimension_semantics=("parallel",),
            vmem_limit_bytes=_vmem_limit(pipelined, single, temps)),
        name="conv_out_proj",
    )(x, attn, proj, proj, proj, proj, proj, conv_w, w_out)


def _sgu_out_proj_kernel(x_ref, u_ref, v_ref, lg_ref, lb_ref, ws_ref, bs_ref, w_ref,
                         o_ref, vn_ref, lhs_ref):
    tm, cw = u_ref.shape
    gw = cw // C_GROUPS
    t_chunk = lax.broadcasted_iota(jnp.int32, (C_BLOCK, C_BLOCK), 0) // CHUNK
    s_chunk = lax.broadcasted_iota(jnp.int32, (C_BLOCK, C_BLOCK), 1) // CHUNK
    causal = s_chunk <= t_chunk
    w_m = [jnp.where(causal, ws_ref[g], 0.0).astype(vn_ref.dtype) for g in range(C_GROUPS)]
    w_out = w_ref[...].astype(lhs_ref.dtype)

    half = tm // 2
    for r0 in range(0, tm, half):
        half_rows = slice(r0, r0 + half)
        v = v_ref[half_rows, :].astype(jnp.float32)
        mu = jnp.mean(v, axis=-1, keepdims=True)
        vc = v - mu
        var = jnp.mean(vc * vc, axis=-1, keepdims=True)
        vn_ref[half_rows, :] = (vc * lax.rsqrt(var + EPS) * lg_ref[...] + lb_ref[...]
                                ).astype(vn_ref.dtype)
        for g in range(C_GROUPS):
            bias = bs_ref[:, g:g + 1]
            cols = slice(g * gw, (g + 1) * gw)
            for n0 in range(r0, r0 + half, C_BLOCK):
                rows = slice(n0, n0 + C_BLOCK)
                s = jnp.dot(w_m[g], vn_ref[rows, cols],
                            preferred_element_type=jnp.float32) + bias
                lhs_ref[rows, cols] = (u_ref[rows, cols].astype(jnp.float32) * s
                                       ).astype(lhs_ref.dtype)
        o_ref[half_rows, :] = x_ref[half_rows, :] + jnp.dot(
            lhs_ref[half_rows, :], w_out, preferred_element_type=jnp.float32)


def _sgu_out_proj(x, z, ln_g, ln_b, w_s, b_s, w_out, index):
    m, d = x.shape
    cw = z.shape[1] // 2
    tm = ROW_TILE_MIX
    assert m % tm == 0 and tm % (2 * C_BLOCK) == 0 and cw % C_GROUPS == 0
    w_shape = w_out.shape[1:]
    pipelined = (2 * _nbytes((tm, d), x.dtype) + 2 * _nbytes((tm, cw), z.dtype)
                 + _nbytes(w_s.shape, w_s.dtype))
    single = _nbytes(w_shape, w_out.dtype) + 2 * _nbytes((tm, cw), _BF16)
    temps = _nbytes(w_shape, _BF16) + 2 * _nbytes((tm // 2, cw), jnp.float32)
    return pl.pallas_call(
        _sgu_out_proj_kernel,
        grid=(m // tm,),
        in_specs=[
            pl.BlockSpec((tm, d), lambda i: (i, 0)),
            pl.BlockSpec((tm, cw), lambda i: (i, 0)),
            pl.BlockSpec((tm, cw), lambda i: (i, 1)),
            pl.BlockSpec((1, cw), lambda i: (0, 0)),
            pl.BlockSpec((1, cw), lambda i: (0, 0)),
            pl.BlockSpec(w_s.shape, lambda i: (0, 0, 0)),
            pl.BlockSpec((C_BLOCK, C_GROUPS), lambda i: (0, 0)),
            pl.BlockSpec((None,) + w_shape, lambda i: (index, 0, 0),
                         pipeline_mode=pl.Buffered(1)),
        ],
        out_specs=pl.BlockSpec((tm, d), lambda i: (i, 0)),
        out_shape=jax.ShapeDtypeStruct((m, d), x.dtype),
        scratch_shapes=[pltpu.VMEM((tm, cw), _BF16), pltpu.VMEM((tm, cw), _BF16)],
        compiler_params=pltpu.CompilerParams(
            dimension_semantics=("parallel",),
            vmem_limit_bytes=_vmem_limit(pipelined, single, temps)),
        name="sgu_out_proj",
    )(x, z, z, ln_g.reshape(1, cw), ln_b.reshape(1, cw), w_s, jnp.transpose(b_s), w_out)


def _ffn_kernel(x_ref, g_ref, wg_ref, wu_ref, wd_ref, fg_ref, o_ref, h_ref, *, final_norm):
    j = pl.program_id(1)
    tf = wg_ref.shape[1]

    def swiglu_tile(h):
        acts = []
        for c0 in range(0, tf, tf // 2):
            cols = slice(c0, c0 + tf // 2)
            gate = jnp.dot(h, wg_ref[:, cols].astype(h.dtype),
                           preferred_element_type=jnp.float32)
            up = jnp.dot(h, wu_ref[:, cols].astype(h.dtype),
                         preferred_element_type=jnp.float32)
            acts.append((gate * jax.nn.sigmoid(gate) * up).astype(h.dtype))
        act = jnp.concatenate(acts, axis=1)
        return jnp.dot(act, wd_ref[...].astype(h.dtype), preferred_element_type=jnp.float32)

    @pl.when(j == 0)
    def _():
        x = x_ref[...]
        h = _rms_norm(x, g_ref[...]).astype(h_ref.dtype)
        h_ref[...] = h
        o_ref[...] = x + swiglu_tile(h)

    @pl.when(j > 0)
    def _():
        o_ref[...] += swiglu_tile(h_ref[...])

    if final_norm:
        @pl.when(j == pl.num_programs(1) - 1)
        def _():
            o_ref[...] = _rms_norm(o_ref[...], fg_ref[...])


def _ffn(x, g, w_gate, w_up, w_down, index, final_g, *, final_norm):
    m, d = x.shape
    f = w_gate.shape[2]
    tm, tf = ROW_TILE_PROJ, COL_TILE_FFN
    assert m % tm == 0 and f % tf == 0
    pipelined = _nbytes((tm, d), x.dtype) + 3 * _nbytes((d, tf), w_gate.dtype)
    single = _nbytes((tm, d), x.dtype) + _nbytes((tm, d), _BF16)
    temps = 2 * _nbytes((d, tf), _BF16) + 2 * _nbytes((tm, tf), jnp.float32)
    return pl.pallas_call(
        functools.partial(_ffn_kernel, final_norm=final_norm),
        grid=(m // tm, f // tf),
        in_specs=[
            pl.BlockSpec((tm, d), lambda i, j: (i, 0), pipeline_mode=pl.Buffered(1)),
            pl.BlockSpec((1, d), lambda i, j: (0, 0)),
            pl.BlockSpec((None, d, tf), lambda i, j: (index, 0, j)),
            pl.BlockSpec((None, d, tf), lambda i, j: (index, 0, j)),
            pl.BlockSpec((None, tf, d), lambda i, j: (index, j, 0)),
            pl.BlockSpec((1, d), lambda i, j: (0, 0)),
        ],
        out_specs=pl.BlockSpec((tm, d), lambda i, j: (i, 0)),
        out_shape=jax.ShapeDtypeStruct((m, d), x.dtype),
        scratch_shapes=[pltpu.VMEM((tm, d), _BF16)],
        compiler_params=pltpu.CompilerParams(
            dimension_semantics=("parallel", "arbitrary"),
            vmem_limit_bytes=_vmem_limit(pipelined, single, temps)),
        name="ffn_final" if final_norm else "ffn",
    )(x, g.reshape(1, d), w_gate, w_up, w_down, final_g.reshape(1, d))


def kernel(x, mix_norm, ab_w_in, ab_rel_bias, ab_conv_w, ab_w_out, c_w_in, c_ln_g, c_ln_b,
           c_w_s, c_b_s, c_w_out, ffn_norm, ffn_w_gate, ffn_w_up, ffn_w_down, final_norm):
    batch, seq, d = x.shape
    depth = mix_norm.shape[0]
    heads = ab_rel_bias.shape[1]

    xf = x.reshape(batch * seq, d)
    bias, w_in = _bias_table(ab_rel_bias.reshape(-1, ab_rel_bias.shape[2]), ab_w_in, 0)
    for layer in range(depth):
        i = layer // 2
        nxt = layer + 1
        next_w = None
        if nxt < depth:
            next_w = (c_w_in if nxt % 2 else ab_w_in, nxt // 2)
        if layer % 2 == 0:
            proj, w_in = _norm_proj(xf, mix_norm[layer], w_in, gelu=False, name="norm_proj",
                                    next_w=next_w)
            attn = _attention(proj, bias, i * heads, batch=batch, seq=seq, heads=heads)
            xf = _conv_out_proj(xf, attn, proj, ab_conv_w[i], ab_w_out, i, seq=seq)
        else:
            z, w_in = _norm_proj(xf, mix_norm[layer], w_in, gelu=True, name="norm_proj_gelu",
                                 next_w=next_w)
            xf = _sgu_out_proj(xf, z, c_ln_g[i], c_ln_b[i], c_w_s[i], c_b_s[i], c_w_out, i)
        xf = _ffn(xf, ffn_norm[layer], ffn_w_gate, ffn_w_up, ffn_w_down, layer, final_norm,
                  final_norm=(layer == depth - 1))
    return xf.reshape(batch, seq, d)
```

```python
import functools

import jax
import jax.numpy as jnp
import numpy as np
from jax import lax
from jax.experimental import pallas as pl
from jax.experimental.pallas import tpu as pltpu

CHUNK = 64
A_HEAD_DIM = 128
A_LEFT_CHUNKS = 8
A_MAX_REL = 256
CONV_WIDTH = 3
C_BLOCK = 128
C_GROUPS = 8
EPS = 1e-6
NEG_INF = -1e30

V7X_VMEM_BYTES = 64 * 1024 * 1024
V7X_BF16_SUBLANES = 16
VMEM_CAP_BYTES = V7X_VMEM_BYTES - 4 * 1024 * 1024
VMEM_SLACK_DIV = 8

ROW_TILE_PROJ = 1024
ROW_TILE_MIX = 512
COL_TILE_PROJ = 2048
COL_TILE_FFN = 512
Q_BLOCK = 2 * CHUNK
K_WINDOW = Q_BLOCK + A_LEFT_CHUNKS * CHUNK
ATTN_INTERLEAVE = 32
ATTN_HEADS = 4

_SQRT_HALF = np.sqrt(0.5).astype(np.float32)
_BF16 = jnp.bfloat16


def _vmem_limit(pipelined_bytes, single_bytes, temp_bytes):
    need = 2 * pipelined_bytes + single_bytes + temp_bytes
    return int(min(need + need // VMEM_SLACK_DIV, VMEM_CAP_BYTES))


def _nbytes(shape, dtype):
    return int(np.prod(shape)) * jnp.dtype(dtype).itemsize


def _rms_norm(x, g):
    ms = jnp.mean(x * x, axis=-1, keepdims=True)
    return (x * lax.rsqrt(ms + EPS)) * g


def _norm_proj_kernel(*refs, gelu, has_next_w):
    if has_next_w:
        x_ref, g_ref, w_ref, next_w_ref, o_ref, next_w16_ref, h_ref = refs
    else:
        x_ref, g_ref, w_ref, o_ref, h_ref = refs

    def project(h):
        if has_next_w:
            next_w16_ref[...] = next_w_ref[...].astype(next_w16_ref.dtype)
        y = jnp.dot(h, w_ref[...], preferred_element_type=jnp.float32)
        if gelu:
            y = 0.5 * y * (1.0 + lax.erf(y * _SQRT_HALF))
        o_ref[...] = y.astype(o_ref.dtype)

    j = pl.program_id(1)

    @pl.when(j == 0)
    def _():
        h = _rms_norm(x_ref[...], g_ref[...]).astype(h_ref.dtype)
        h_ref[...] = h
        project(h)

    @pl.when(j > 0)
    def _():
        project(h_ref[...])


def _norm_proj(x, g, w, *, gelu, name, next_w=None):
    m, d = x.shape
    n = w.shape[1]
    tm, tn = ROW_TILE_PROJ, COL_TILE_PROJ
    assert m % tm == 0 and n % tn == 0
    grid = (m // tm, n // tn)
    in_specs = [
        pl.BlockSpec((tm, d), lambda i, j: (i, 0)),
        pl.BlockSpec((1, d), lambda i, j: (0, 0)),
        pl.BlockSpec((d, tn), lambda i, j: (0, j)),
    ]
    out_specs = [pl.BlockSpec((tm, tn), lambda i, j: (i, j))]
    out_shape = [jax.ShapeDtypeStruct((m, n), _BF16)]
    operands = [x, g.reshape(1, d), w]
    cast_bytes = 0
    if next_w is not None:
        stacked, index = next_w
        _, r, c = stacked.shape
        n_steps = grid[0] * grid[1]
        n_blocks = max(k for k in range(1, n_steps + 1) if r % (k * V7X_BF16_SUBLANES) == 0)
        rows = r // n_blocks

        def block_of(i, j):
            return jnp.minimum(i * grid[1] + j, n_blocks - 1)

        in_specs.append(pl.BlockSpec((None, rows, c), lambda i, j: (index, block_of(i, j), 0)))
        out_specs.append(pl.BlockSpec((rows, c), lambda i, j: (block_of(i, j), 0)))
        out_shape.append(jax.ShapeDtypeStruct((r, c), _BF16))
        operands.append(stacked)
        cast_bytes = _nbytes((rows, c), stacked.dtype) + _nbytes((rows, c), _BF16)
    pipelined = (_nbytes((tm, d), x.dtype) + _nbytes((d, tn), w.dtype)
                 + _nbytes((tm, tn), _BF16) + cast_bytes)
    scratch = _nbytes((tm, d), _BF16)
    temps = _nbytes((tm, tn), jnp.float32)
    outs = pl.pallas_call(
        functools.partial(_norm_proj_kernel, gelu=gelu, has_next_w=next_w is not None),
        grid=grid,
        in_specs=in_specs,
        out_specs=out_specs,
        out_shape=out_shape,
        scratch_shapes=[pltpu.VMEM((tm, d), _BF16)],
        compiler_params=pltpu.CompilerParams(
            dimension_semantics=("arbitrary", "arbitrary"),
            vmem_limit_bytes=_vmem_limit(pipelined, scratch, temps)),
        name=name,
    )(*operands)
    return outs[0], (outs[1] if next_w is not None else None)


def _bias_table_kernel(g_ref, w_ref, o_ref, w16_ref):
    w16_ref[...] = w_ref[...].astype(w16_ref.dtype)
    width = Q_BLOCK + K_WINDOW
    x = jnp.broadcast_to(g_ref[0], (Q_BLOCK, width))
    row = lax.broadcasted_iota(jnp.int32, (Q_BLOCK, width), 0)
    shift = 1
    while shift < Q_BLOCK:
        x = jnp.where((row & shift) != 0, pltpu.roll(x, shift, 1), x)
        shift *= 2
    t = x[:, Q_BLOCK:]
    q_chunk = lax.broadcasted_iota(jnp.int32, (Q_BLOCK, K_WINDOW), 0) // CHUNK
    k_chunk = lax.broadcasted_iota(jnp.int32, (Q_BLOCK, K_WINDOW), 1) // CHUNK
    band = k_chunk - q_chunk
    o_ref[0] = jnp.where((band >= 0) & (band <= A_LEFT_CHUNKS), t, NEG_INF)


def _bias_table(rel_bias, first_w, index):
    heads, n_rel = rel_bias.shape
    width = Q_BLOCK + K_WINDOW
    _, r, c = first_w.shape
    assert r % (heads * V7X_BF16_SUBLANES) == 0
    rows = r // heads
    left = Q_BLOCK + A_LEFT_CHUNKS * CHUNK - A_MAX_REL
    right = max(0, width - left - n_rel)
    g = jnp.pad(rel_bias[:, ::-1], ((0, 0), (left, right)), mode="edge")[:, :width]
    pipelined = (_nbytes((Q_BLOCK, K_WINDOW), jnp.float32)
                 + _nbytes((rows, c), first_w.dtype) + _nbytes((rows, c), _BF16))
    return pl.pallas_call(
        _bias_table_kernel,
        grid=(heads,),
        in_specs=[pl.BlockSpec((1, 1, width), lambda h: (h, 0, 0)),
                  pl.BlockSpec((None, rows, c), lambda h: (index, h, 0))],
        out_specs=[pl.BlockSpec((1, Q_BLOCK, K_WINDOW), lambda h: (h, 0, 0)),
                   pl.BlockSpec((rows, c), lambda h: (h, 0))],
        out_shape=[jax.ShapeDtypeStruct((heads, Q_BLOCK, K_WINDOW), jnp.float32),
                   jax.ShapeDtypeStruct((r, c), _BF16)],
        compiler_params=pltpu.CompilerParams(
            dimension_semantics=("parallel",),
            vmem_limit_bytes=_vmem_limit(
                pipelined, _nbytes((r, c), _BF16),
                6 * _nbytes((Q_BLOCK, width), jnp.float32))),
        name="bias_table",
    )(g.reshape(heads, 1, width), first_w)


def _attention_kernel(q_ref, k_ref, v_ref, b_ref, o_ref, kt_ref, v1_ref):
    seq = q_ref.shape[0]
    dh = A_HEAD_DIM
    kt_ref[...] = k_ref[...].T
    blocks = []
    for hd in range(ATTN_HEADS):
        cols = slice(hd * dh, (hd + 1) * dh)
        v1_ref[hd, :, :dh] = v_ref[:, cols]
        v1_ref[hd, :, dh:] = jnp.ones((seq, dh), v1_ref.dtype)
        for q0 in range(0, seq, Q_BLOCK):
            blocks.append((hd, cols, q0, max(0, q0 + Q_BLOCK - K_WINDOW), q0 + Q_BLOCK))
    for g0 in range(0, len(blocks), ATTN_INTERLEAVE):
        group = blocks[g0:g0 + ATTN_INTERLEAVE]
        s = [jnp.dot(q_ref[q0:k1, cols], kt_ref[cols, k0:k1],
                     preferred_element_type=jnp.float32) for hd, cols, q0, k0, k1 in group]
        s = [si * (dh ** -0.5) + b_ref[hd, :, K_WINDOW - (k1 - k0):]
             for si, (hd, cols, q0, k0, k1) in zip(s, group)]
        m = [jnp.max(si, axis=-1, keepdims=True) for si in s]
        p = [jnp.exp(si - mi).astype(v1_ref.dtype) for si, mi in zip(s, m)]
        o = [jnp.dot(pi, v1_ref[hd, k0:k1, :], preferred_element_type=jnp.float32)
             for pi, (hd, cols, q0, k0, k1) in zip(p, group)]
        for oi, (hd, cols, q0, k0, k1) in zip(o, group):
            o_ref[q0:k1, cols] = (oi[:, :dh] * (1.0 / oi[:, dh:dh + 1])).astype(o_ref.dtype)


def _attention(proj, bias, table0, *, batch, seq, heads):
    dh = A_HEAD_DIM
    hs = ATTN_HEADS
    assert seq % Q_BLOCK == 0 and K_WINDOW % Q_BLOCK == 0
    assert (hs * seq // Q_BLOCK) % ATTN_INTERLEAVE == 0 and heads % hs == 0 and table0 % hs == 0
    n_hb = heads // hs
    pipelined = (4 * _nbytes((seq, hs * dh), proj.dtype)
                 + _nbytes((hs, Q_BLOCK, K_WINDOW), jnp.float32))
    scratch = 3 * _nbytes((seq, hs * dh), proj.dtype)
    temps = 2 * ATTN_INTERLEAVE * _nbytes((Q_BLOCK, K_WINDOW), jnp.float32)
    return pl.pallas_call(
        _attention_kernel,
        grid=(batch, n_hb),
        in_specs=[
            pl.BlockSpec((seq, hs * dh), lambda b, h: (b, h)),
            pl.BlockSpec((seq, hs * dh), lambda b, h: (b, n_hb + h)),
            pl.BlockSpec((seq, hs * dh), lambda b, h: (b, 2 * n_hb + h)),
            pl.BlockSpec((hs, Q_BLOCK, K_WINDOW), lambda b, h: (table0 // hs + h, 0, 0)),
        ],
        out_specs=pl.BlockSpec((seq, hs * dh), lambda b, h: (b, h)),
        out_shape=jax.ShapeDtypeStruct((batch * seq, heads * dh), proj.dtype),
        scratch_shapes=[pltpu.VMEM((hs * dh, seq), proj.dtype),
                        pltpu.VMEM((hs, seq, 2 * dh), proj.dtype)],
        compiler_params=pltpu.CompilerParams(
            dimension_semantics=("parallel", "parallel"),
            vmem_limit_bytes=_vmem_limit(pipelined, scratch, temps)),
        name="band_attention",
    )(proj, proj, proj, bias)


def _conv_out_proj_kernel(x_ref, a_ref, bg_ref, cg_ref, hv_ref, cprev_ref, hprev_ref,
                          cw_ref, w_ref, o_ref, z_ref, lhs_ref, *, rows_per_seq):
    tm, aw = a_ref.shape
    halo = cprev_ref.shape[0]
    f32 = jnp.float32
    i = pl.program_id(0)
    at_seq_start = (i * tm) % rows_per_seq == 0
    zprev = cprev_ref[...].astype(f32) * hprev_ref[...].astype(f32)
    z_ref[:halo] = jnp.where(at_seq_start, 0.0, zprev)
    z_ref[halo:] = cg_ref[...].astype(f32) * hv_ref[...].astype(f32)
    y = cw_ref[0:1] * z_ref[pl.ds(halo - 2, tm)]
    y = y + cw_ref[1:2] * z_ref[pl.ds(halo - 1, tm)]
    y = y + cw_ref[2:3] * z_ref[pl.ds(halo, tm)]
    lhs_ref[:, :aw] = a_ref[...]
    lhs_ref[:, aw:] = (bg_ref[...].astype(f32) * y).astype(lhs_ref.dtype)
    o_ref[...] = x_ref[...] + jnp.dot(lhs_ref[...], w_ref[...].astype(lhs_ref.dtype),
                                      preferred_element_type=f32)


def _conv_out_proj(x, attn, proj, conv_w, w_out, index, *, seq):
    m, d = x.shape
    aw = attn.shape[1]
    bw = conv_w.shape[1]
    tm = ROW_TILE_MIX
    halo = V7X_BF16_SUBLANES
    assert m % tm == 0 and seq % tm == 0 and CONV_WIDTH - 1 <= halo
    assert conv_w.shape == (CONV_WIDTH, bw) and CONV_WIDTH == 3
    gate0 = (proj.shape[1] - 3 * bw) // bw
    assert gate0 * bw + 3 * bw == proj.shape[1]
    halo_blocks = tm // halo
    w_shape = w_out.shape[1:]

    def prev_rows(col):
        return pl.BlockSpec(
            (halo, bw), lambda i: (jnp.maximum(i * halo_blocks - 1, 0), col))

    pipelined = (2 * _nbytes((tm, d), x.dtype) + _nbytes((tm, aw), attn.dtype)
                 + 3 * _nbytes((tm, bw), proj.dtype))
    single = (_nbytes(w_shape, w_out.dtype) + _nbytes((tm + halo, bw), jnp.float32)
              + _nbytes((tm, aw + bw), _BF16))
    temps = _nbytes(w_shape, _BF16) + 2 * _nbytes((tm, bw), jnp.float32)
    return pl.pallas_call(
        functools.partial(_conv_out_proj_kernel, rows_per_seq=seq),
        grid=(m // tm,),
        in_specs=[
            pl.BlockSpec((tm, d), lambda i: (i, 0)),
            pl.BlockSpec((tm, aw), lambda i: (i, 0)),
            pl.BlockSpec((tm, bw), lambda i: (i, gate0)),
            pl.BlockSpec((tm, bw), lambda i: (i, gate0 + 1)),
            pl.BlockSpec((tm, bw), lambda i: (i, gate0 + 2)),
            prev_rows(gate0 + 1),
            prev_rows(gate0 + 2),
            pl.BlockSpec((CONV_WIDTH, bw), lambda i: (0, 0)),
            pl.BlockSpec((None,) + w_shape, lambda i: (index, 0, 0),
                         pipeline_mode=pl.Buffered(1)),
        ],
        out_specs=pl.BlockSpec((tm, d), lambda i: (i, 0)),
        out_shape=jax.ShapeDtypeStruct((m, d), x.dtype),
        scratch_shapes=[pltpu.VMEM((tm + halo, bw), jnp.float32),
                        pltpu.VMEM((tm, aw + bw), _BF16)],
        compiler_params=pltpu.CompilerParams(
            dimension_semantics=("parallel",),
            vmem_limit_bytes=_vmem_limit(pipelined, single, temps)),
        name="conv_out_proj",
    )(x, attn, proj, proj, proj, proj, proj, conv_w, w_out)


def _sgu_out_proj_kernel(x_ref, u_ref, v_ref, lg_ref, lb_ref, ws_ref, bs_ref, w_ref,
                         o_ref, vn_ref, lhs_ref):
    tm, cw = u_ref.shape
    gw = cw // C_GROUPS
    t_chunk = lax.broadcasted_iota(jnp.int32, (C_BLOCK, C_BLOCK), 0) // CHUNK
    s_chunk = lax.broadcasted_iota(jnp.int32, (C_BLOCK, C_BLOCK), 1) // CHUNK
    causal = s_chunk <= t_chunk
    w_m = [jnp.where(causal, ws_ref[g], 0.0).astype(vn_ref.dtype) for g in range(C_GROUPS)]
    w_out = w_ref[...].astype(lhs_ref.dtype)

    half = tm // 2
    for r0 in range(0, tm, half):
        half_rows = slice(r0, r0 + half)
        v = v_ref[half_rows, :].astype(jnp.float32)
        mu = jnp.mean(v, axis=-1, keepdims=True)
        vc = v - mu
        var = jnp.mean(vc * vc, axis=-1, keepdims=True)
        vn_ref[half_rows, :] = (vc * lax.rsqrt(var + EPS) * lg_ref[...] + lb_ref[...]
                                ).astype(vn_ref.dtype)
        for g in range(C_GROUPS):
            bias = bs_ref[:, g:g + 1]
            cols = slice(g * gw, (g + 1) * gw)
            for n0 in range(r0, r0 + half, C_BLOCK):
                rows = slice(n0, n0 + C_BLOCK)
                s = jnp.dot(w_m[g], vn_ref[rows, cols],
                            preferred_element_type=jnp.float32) + bias
                lhs_ref[rows, cols] = (u_ref[rows, cols].astype(jnp.float32) * s
                                       ).astype(lhs_ref.dtype)
        o_ref[half_rows, :] = x_ref[half_rows, :] + jnp.dot(
            lhs_ref[half_rows, :], w_out, preferred_element_type=jnp.float32)


def _sgu_out_proj(x, z, ln_g, ln_b, w_s, b_s, w_out, index):
    m, d = x.shape
    cw = z.shape[1] // 2
    tm = ROW_TILE_MIX
    assert m % tm == 0 and tm % (2 * C_BLOCK) == 0 and cw % C_GROUPS == 0
    w_shape = w_out.shape[1:]
    pipelined = (2 * _nbytes((tm, d), x.dtype) + 2 * _nbytes((tm, cw), z.dtype)
                 + _nbytes(w_s.shape, w_s.dtype))
    single = _nbytes(w_shape, w_out.dtype) + 2 * _nbytes((tm, cw), _BF16)
    temps = _nbytes(w_shape, _BF16) + 2 * _nbytes((tm // 2, cw), jnp.float32)
    return pl.pallas_call(
        _sgu_out_proj_kernel,
        grid=(m // tm,),
        in_specs=[
            pl.BlockSpec((tm, d), lambda i: (i, 0)),
            pl.BlockSpec((tm, cw), lambda i: (i, 0)),
            pl.BlockSpec((tm, cw), lambda i: (i, 1)),
            pl.BlockSpec((1, cw), lambda i: (0, 0)),
            pl.BlockSpec((1, cw), lambda i: (0, 0)),
            pl.BlockSpec(w_s.shape, lambda i: (0, 0, 0)),
            pl.BlockSpec((C_BLOCK, C_GROUPS), lambda i: (0, 0)),
            pl.BlockSpec((None,) + w_shape, lambda i: (index, 0, 0),
                         pipeline_mode=pl.Buffered(1)),
        ],
        out_specs=pl.BlockSpec((tm, d), lambda i: (i, 0)),
        out_shape=jax.ShapeDtypeStruct((m, d), x.dtype),
        scratch_shapes=[pltpu.VMEM((tm, cw), _BF16), pltpu.VMEM((tm, cw), _BF16)],
        compiler_params=pltpu.CompilerParams(
            dimension_semantics=("parallel",),
            vmem_limit_bytes=_vmem_limit(pipelined, single, temps)),
        name="sgu_out_proj",
    )(x, z, z, ln_g.reshape(1, cw), ln_b.reshape(1, cw), w_s, jnp.transpose(b_s), w_out)


def _ffn_kernel(x_hbm_ref, g_ref, wg_ref, wu_ref, wd_ref, fg_ref, o_ref, h_ref, x_ref,
                x_sem, *, final_norm):
    i, j = pl.program_id(0), pl.program_id(1)
    tm = x_ref.shape[0]
    tf = wg_ref.shape[1]

    def x_copy(tile):
        return pltpu.make_async_copy(x_hbm_ref.at[pl.ds(tile * tm, tm), :], x_ref, x_sem)

    @pl.when((i == 0) & (j == 0))
    def _():
        x_copy(0).start()

    @pl.when((j == 1) & (i + 1 < pl.num_programs(0)))
    def _():
        x_copy(i + 1).start()

    def swiglu_tile(h):
        acts = []
        for c0 in range(0, tf, tf // 2):
            cols = slice(c0, c0 + tf // 2)
            gate = jnp.dot(h, wg_ref[:, cols].astype(h.dtype),
                           preferred_element_type=jnp.float32)
            up = jnp.dot(h, wu_ref[:, cols].astype(h.dtype),
                         preferred_element_type=jnp.float32)
            acts.append((gate * jax.nn.sigmoid(gate) * up).astype(h.dtype))
        act = jnp.concatenate(acts, axis=1)
        return jnp.dot(act, wd_ref[...].astype(h.dtype), preferred_element_type=jnp.float32)

    @pl.when(j == 0)
    def _():
        x_copy(i).wait()
        x = x_ref[...]
        h = _rms_norm(x, g_ref[...]).astype(h_ref.dtype)
        h_ref[...] = h
        o_ref[...] = x + swiglu_tile(h)

    @pl.when(j > 0)
    def _():
        o_ref[...] += swiglu_tile(h_ref[...])

    if final_norm:
        @pl.when(j == pl.num_programs(1) - 1)
        def _():
            o_ref[...] = _rms_norm(o_ref[...], fg_ref[...])


def _ffn(x, g, w_gate, w_up, w_down, index, final_g, *, final_norm):
    m, d = x.shape
    f = w_gate.shape[2]
    tm, tf = ROW_TILE_PROJ, COL_TILE_FFN
    assert m % tm == 0 and f % tf == 0
    pipelined = _nbytes((tm, d), x.dtype) + 3 * _nbytes((d, tf), w_gate.dtype)
    single = _nbytes((tm, d), x.dtype) + _nbytes((tm, d), _BF16)
    temps = _nbytes((d, tf), _BF16) + _nbytes((tm, tf), jnp.float32)
    return pl.pallas_call(
        functools.partial(_ffn_kernel, final_norm=final_norm),
        grid=(m // tm, f // tf),
        in_specs=[
            pl.BlockSpec(memory_space=pl.ANY),
            pl.BlockSpec((1, d), lambda i, j: (0, 0)),
            pl.BlockSpec((None, d, tf), lambda i, j: (index, 0, j)),
            pl.BlockSpec((None, d, tf), lambda i, j: (index, 0, j)),
            pl.BlockSpec((None, tf, d), lambda i, j: (index, j, 0)),
            pl.BlockSpec((1, d), lambda i, j: (0, 0)),
        ],
        out_specs=pl.BlockSpec((tm, d), lambda i, j: (i, 0)),
        out_shape=jax.ShapeDtypeStruct((m, d), x.dtype),
        scratch_shapes=[pltpu.VMEM((tm, d), _BF16), pltpu.VMEM((tm, d), x.dtype),
                        pltpu.SemaphoreType.DMA(())],
        compiler_params=pltpu.CompilerParams(
            dimension_semantics=("arbitrary", "arbitrary"),
            vmem_limit_bytes=_vmem_limit(pipelined, single, temps)),
        name="ffn_final" if final_norm else "ffn",
    )(x, g.reshape(1, d), w_gate, w_up, w_down, final_g.reshape(1, d))


def kernel(x, mix_norm, ab_w_in, ab_rel_bias, ab_conv_w, ab_w_out, c_w_in, c_ln_g, c_ln_b,
           c_w_s, c_b_s, c_w_out, ffn_norm, ffn_w_gate, ffn_w_up, ffn_w_down, final_norm):
    batch, seq, d = x.shape
    depth = mix_norm.shape[0]
    heads = ab_rel_bias.shape[1]

    xf = x.reshape(batch * seq, d)
    bias, w_in = _bias_table(ab_rel_bias.reshape(-1, ab_rel_bias.shape[2]), ab_w_in, 0)
    for layer in range(depth):
        i = layer // 2
        nxt = layer + 1
        next_w = None
        if nxt < depth:
            next_w = (c_w_in if nxt % 2 else ab_w_in, nxt // 2)
        if layer % 2 == 0:
            proj, w_in = _norm_proj(xf, mix_norm[layer], w_in, gelu=False, name="norm_proj",
                                    next_w=next_w)
            attn = _attention(proj, bias, i * heads, batch=batch, seq=seq, heads=heads)
            xf = _conv_out_proj(xf, attn, proj, ab_conv_w[i], ab_w_out, i, seq=seq)
        else:
            z, w_in = _norm_proj(xf, mix_norm[layer], w_in, gelu=True, name="norm_proj_gelu",
                                 next_w=next_w)
            xf = _sgu_out_proj(xf, z, c_ln_g[i], c_ln_b[i], c_w_s[i], c_b_s[i], c_w_out, i)
        xf = _ffn(xf, ffn_norm[layer], ffn_w_gate, ffn_w_up, ffn_w_down, layer, final_norm,
                  final_norm=(layer == depth - 1))
    return xf.reshape(batch, seq, d)
```

```python
import functools

import jax
import jax.numpy as jnp
import numpy as np
from jax import lax
from jax.experimental import pallas as pl
from jax.experimental.pallas import tpu as pltpu

CHUNK = 64
A_HEAD_DIM = 128
A_LEFT_CHUNKS = 8
A_MAX_REL = 256
CONV_WIDTH = 3
C_BLOCK = 128
C_GROUPS = 8
EPS = 1e-6
NEG_INF = -1e30

V7X_VMEM_BYTES = 64 * 1024 * 1024
V7X_BF16_SUBLANES = 16
VMEM_CAP_BYTES = V7X_VMEM_BYTES - 4 * 1024 * 1024
VMEM_SLACK_DIV = 8

ROW_TILE_PROJ = 1024
ROW_TILE_MIX = 512
COL_TILE_PROJ = 2048
COL_TILE_FFN = 512
Q_BLOCK = 2 * CHUNK
K_WINDOW = Q_BLOCK + A_LEFT_CHUNKS * CHUNK
ATTN_INTERLEAVE = 16
ATTN_HEADS = 4

_SQRT_HALF = np.sqrt(0.5).astype(np.float32)
_BF16 = jnp.bfloat16


def _vmem_limit(pipelined_bytes, single_bytes, temp_bytes):
    need = 2 * pipelined_bytes + single_bytes + temp_bytes
    return int(min(need + need // VMEM_SLACK_DIV, VMEM_CAP_BYTES))


def _nbytes(shape, dtype):
    return int(np.prod(shape)) * jnp.dtype(dtype).itemsize


def _rms_norm(x, g):
    ms = jnp.mean(x * x, axis=-1, keepdims=True)
    return (x * lax.rsqrt(ms + EPS)) * g


def _norm_proj_kernel(*refs, gelu, has_next_w):
    if has_next_w:
        x_ref, g_ref, w_ref, next_w_ref, o_ref, next_w16_ref, h_ref = refs
    else:
        x_ref, g_ref, w_ref, o_ref, h_ref = refs

    def project(h):
        if has_next_w:
            next_w16_ref[...] = next_w_ref[...].astype(next_w16_ref.dtype)
        y = jnp.dot(h, w_ref[...], preferred_element_type=jnp.float32)
        if gelu:
            y = 0.5 * y * (1.0 + lax.erf(y * _SQRT_HALF))
        o_ref[...] = y.astype(o_ref.dtype)

    j = pl.program_id(1)

    @pl.when(j == 0)
    def _():
        h = _rms_norm(x_ref[...], g_ref[...]).astype(h_ref.dtype)
        h_ref[...] = h
        project(h)

    @pl.when(j > 0)
    def _():
        project(h_ref[...])


def _norm_proj(x, g, w, *, gelu, name, next_w=None):
    m, d = x.shape
    n = w.shape[1]
    tm, tn = ROW_TILE_PROJ, COL_TILE_PROJ
    assert m % tm == 0 and n % tn == 0
    grid = (m // tm, n // tn)
    in_specs = [
        pl.BlockSpec((tm, d), lambda i, j: (i, 0)),
        pl.BlockSpec((1, d), lambda i, j: (0, 0)),
        pl.BlockSpec((d, tn), lambda i, j: (0, j)),
    ]
    out_specs = [pl.BlockSpec((tm, tn), lambda i, j: (i, j))]
    out_shape = [jax.ShapeDtypeStruct((m, n), _BF16)]
    operands = [x, g.reshape(1, d), w]
    cast_bytes = 0
    if next_w is not None:
        stacked, index = next_w
        _, r, c = stacked.shape
        n_steps = grid[0] * grid[1]
        n_blocks = max(k for k in range(1, n_steps + 1) if r % (k * V7X_BF16_SUBLANES) == 0)
        rows = r // n_blocks

        def block_of(i, j):
            return jnp.minimum(i * grid[1] + j, n_blocks - 1)

        in_specs.append(pl.BlockSpec((None, rows, c), lambda i, j: (index, block_of(i, j), 0)))
        out_specs.append(pl.BlockSpec((rows, c), lambda i, j: (block_of(i, j), 0)))
        out_shape.append(jax.ShapeDtypeStruct((r, c), _BF16))
        operands.append(stacked)
        cast_bytes = _nbytes((rows, c), stacked.dtype) + _nbytes((rows, c), _BF16)
    pipelined = (_nbytes((tm, d), x.dtype) + _nbytes((d, tn), w.dtype)
                 + _nbytes((tm, tn), _BF16) + cast_bytes)
    scratch = _nbytes((tm, d), _BF16)
    temps = _nbytes((tm, tn), jnp.float32)
    outs = pl.pallas_call(
        functools.partial(_norm_proj_kernel, gelu=gelu, has_next_w=next_w is not None),
        grid=grid,
        in_specs=in_specs,
        out_specs=out_specs,
        out_shape=out_shape,
        scratch_shapes=[pltpu.VMEM((tm, d), _BF16)],
        compiler_params=pltpu.CompilerParams(
            dimension_semantics=("arbitrary", "arbitrary"),
            vmem_limit_bytes=_vmem_limit(pipelined, scratch, temps)),
        name=name,
    )(*operands)
    return outs[0], (outs[1] if next_w is not None else None)


def _bias_table_kernel(g_ref, w_ref, o_ref, w16_ref):
    w16_ref[...] = w_ref[...].astype(w16_ref.dtype)
    width = Q_BLOCK + K_WINDOW
    x = jnp.broadcast_to(g_ref[0], (Q_BLOCK, width))
    row = lax.broadcasted_iota(jnp.int32, (Q_BLOCK, width), 0)
    shift = 1
    while shift < Q_BLOCK:
        x = jnp.where((row & shift) != 0, pltpu.roll(x, shift, 1), x)
        shift *= 2
    t = x[:, Q_BLOCK:]
    q_chunk = lax.broadcasted_iota(jnp.int32, (Q_BLOCK, K_WINDOW), 0) // CHUNK
    k_chunk = lax.broadcasted_iota(jnp.int32, (Q_BLOCK, K_WINDOW), 1) // CHUNK
    band = k_chunk - q_chunk
    o_ref[0] = jnp.where((band >= 0) & (band <= A_LEFT_CHUNKS), t, NEG_INF)


def _bias_table(rel_bias, first_w, index):
    heads, n_rel = rel_bias.shape
    width = Q_BLOCK + K_WINDOW
    _, r, c = first_w.shape
    assert r % (heads * V7X_BF16_SUBLANES) == 0
    rows = r // heads
    left = Q_BLOCK + A_LEFT_CHUNKS * CHUNK - A_MAX_REL
    right = max(0, width - left - n_rel)
    g = jnp.pad(rel_bias[:, ::-1], ((0, 0), (left, right)), mode="edge")[:, :width]
    pipelined = (_nbytes((Q_BLOCK, K_WINDOW), jnp.float32)
                 + _nbytes((rows, c), first_w.dtype) + _nbytes((rows, c), _BF16))
    return pl.pallas_call(
        _bias_table_kernel,
        grid=(heads,),
        in_specs=[pl.BlockSpec((1, 1, width), lambda h: (h, 0, 0)),
                  pl.BlockSpec((None, rows, c), lambda h: (index, h, 0))],
        out_specs=[pl.BlockSpec((1, Q_BLOCK, K_WINDOW), lambda h: (h, 0, 0)),
                   pl.BlockSpec((rows, c), lambda h: (h, 0))],
        out_shape=[jax.ShapeDtypeStruct((heads, Q_BLOCK, K_WINDOW), jnp.float32),
                   jax.ShapeDtypeStruct((r, c), _BF16)],
        compiler_params=pltpu.CompilerParams(
            dimension_semantics=("parallel",),
            vmem_limit_bytes=_vmem_limit(
                pipelined, _nbytes((r, c), _BF16),
                6 * _nbytes((Q_BLOCK, width), jnp.float32))),
        name="bias_table",
    )(g.reshape(heads, 1, width), first_w)


def _attention_kernel(q_ref, k_ref, v_ref, b_ref, wa_ref, wc_ref, o_ref, wa16_ref, wc16_ref,
                      kt_ref, v1_ref):
    wa16_ref[...] = wa_ref[...].astype(wa16_ref.dtype)
    wc16_ref[...] = wc_ref[...].astype(wc16_ref.dtype)
    seq = q_ref.shape[0]
    dh = A_HEAD_DIM
    kt_ref[...] = k_ref[...].T
    blocks = []
    for hd in range(ATTN_HEADS):
        cols = slice(hd * dh, (hd + 1) * dh)
        v1_ref[hd, :, :dh] = v_ref[:, cols]
        v1_ref[hd, :, dh:] = jnp.ones((seq, dh), v1_ref.dtype)
        for q0 in range(0, seq, Q_BLOCK):
            blocks.append((hd, cols, q0, max(0, q0 + Q_BLOCK - K_WINDOW), q0 + Q_BLOCK))
    for g0 in range(0, len(blocks), ATTN_INTERLEAVE):
        group = blocks[g0:g0 + ATTN_INTERLEAVE]
        s = [jnp.dot(q_ref[q0:k1, cols], kt_ref[cols, k0:k1],
                     preferred_element_type=jnp.float32) for hd, cols, q0, k0, k1 in group]
        s = [si * (dh ** -0.5) + b_ref[hd, :, K_WINDOW - (k1 - k0):]
             for si, (hd, cols, q0, k0, k1) in zip(s, group)]
        m = [jnp.max(si, axis=-1, keepdims=True) for si in s]
        p = [jnp.exp(si - mi).astype(v1_ref.dtype) for si, mi in zip(s, m)]
        o = [jnp.dot(pi, v1_ref[hd, k0:k1, :], preferred_element_type=jnp.float32)
             for pi, (hd, cols, q0, k0, k1) in zip(p, group)]
        for oi, (hd, cols, q0, k0, k1) in zip(o, group):
            o_ref[q0:k1, cols] = (oi[:, :dh] * (1.0 / oi[:, dh:dh + 1])).astype(o_ref.dtype)


def _attention(proj, bias, table0, w_a, w_c, index, *, batch, seq, heads):
    dh = A_HEAD_DIM
    hs = ATTN_HEADS
    assert seq % Q_BLOCK == 0 and K_WINDOW % Q_BLOCK == 0
    assert (hs * seq // Q_BLOCK) % ATTN_INTERLEAVE == 0 and heads % hs == 0 and table0 % hs == 0
    n_hb = heads // hs
    n_steps = batch * n_hb
    _, r, c = w_a.shape
    assert w_c.shape[1:] == (r, c) and r % (n_steps * V7X_BF16_SUBLANES) == 0
    rows = r // n_steps
    w_in_spec = pl.BlockSpec((None, rows, c), lambda b, h: (index, b * n_hb + h, 0))
    w_out_spec = pl.BlockSpec((rows, c), lambda b, h: (b * n_hb + h, 0))
    pipelined = (4 * _nbytes((seq, hs * dh), proj.dtype)
                 + _nbytes((hs, Q_BLOCK, K_WINDOW), jnp.float32)
                 + 2 * (_nbytes((rows, c), w_a.dtype) + _nbytes((rows, c), _BF16)))
    scratch = 3 * _nbytes((seq, hs * dh), proj.dtype)
    temps = 2 * ATTN_INTERLEAVE * _nbytes((Q_BLOCK, K_WINDOW), jnp.float32)
    return pl.pallas_call(
        _attention_kernel,
        grid=(batch, n_hb),
        in_specs=[
            pl.BlockSpec((seq, hs * dh), lambda b, h: (b, h)),
            pl.BlockSpec((seq, hs * dh), lambda b, h: (b, n_hb + h)),
            pl.BlockSpec((seq, hs * dh), lambda b, h: (b, 2 * n_hb + h)),
            pl.BlockSpec((hs, Q_BLOCK, K_WINDOW), lambda b, h: (table0 // hs + h, 0, 0)),
            w_in_spec,
            w_in_spec,
        ],
        out_specs=[pl.BlockSpec((seq, hs * dh), lambda b, h: (b, h)), w_out_spec, w_out_spec],
        out_shape=[jax.ShapeDtypeStruct((batch * seq, heads * dh), proj.dtype),
                   jax.ShapeDtypeStruct((r, c), _BF16), jax.ShapeDtypeStruct((r, c), _BF16)],
        scratch_shapes=[pltpu.VMEM((hs * dh, seq), proj.dtype),
                        pltpu.VMEM((hs, seq, 2 * dh), proj.dtype)],
        compiler_params=pltpu.CompilerParams(
            dimension_semantics=("parallel", "parallel"),
            vmem_limit_bytes=_vmem_limit(pipelined, scratch, temps)),
        name="band_attention",
    )(proj, proj, proj, bias, w_a, w_c)


def _conv_out_proj_kernel(x_ref, a_ref, bg_ref, cg_ref, hv_ref, cprev_ref, hprev_ref,
                          cw_ref, w_ref, o_ref, z_ref, lhs_ref, *, rows_per_seq):
    tm, aw = a_ref.shape
    halo = cprev_ref.shape[0]
    f32 = jnp.float32
    i = pl.program_id(0)
    at_seq_start = (i * tm) % rows_per_seq == 0
    zprev = cprev_ref[...].astype(f32) * hprev_ref[...].astype(f32)
    z_ref[:halo] = jnp.where(at_seq_start, 0.0, zprev)
    z_ref[halo:] = cg_ref[...].astype(f32) * hv_ref[...].astype(f32)
    y = cw_ref[0:1] * z_ref[pl.ds(halo - 2, tm)]
    y = y + cw_ref[1:2] * z_ref[pl.ds(halo - 1, tm)]
    y = y + cw_ref[2:3] * z_ref[pl.ds(halo, tm)]
    lhs_ref[:, :aw] = a_ref[...]
    lhs_ref[:, aw:] = (bg_ref[...].astype(f32) * y).astype(lhs_ref.dtype)
    o_ref[...] = x_ref[...] + jnp.dot(lhs_ref[...], w_ref[...], preferred_element_type=f32)


def _conv_out_proj(x, attn, proj, conv_w, w_out, *, seq):
    m, d = x.shape
    aw = attn.shape[1]
    bw = conv_w.shape[1]
    tm = ROW_TILE_MIX
    halo = V7X_BF16_SUBLANES
    assert m % tm == 0 and seq % tm == 0 and CONV_WIDTH - 1 <= halo
    assert conv_w.shape == (CONV_WIDTH, bw) and CONV_WIDTH == 3
    gate0 = (proj.shape[1] - 3 * bw) // bw
    assert gate0 * bw + 3 * bw == proj.shape[1]
    halo_blocks = tm // halo

    def prev_rows(col):
        return pl.BlockSpec(
            (halo, bw), lambda i: (jnp.maximum(i * halo_blocks - 1, 0), col))

    pipelined = (2 * _nbytes((tm, d), x.dtype) + _nbytes((tm, aw), attn.dtype)
                 + 3 * _nbytes((tm, bw), proj.dtype))
    single = (_nbytes(w_out.shape, w_out.dtype) + _nbytes((tm + halo, bw), jnp.float32)
              + _nbytes((tm, aw + bw), _BF16))
    temps = 2 * _nbytes((tm, bw), jnp.float32)
    return pl.pallas_call(
        functools.partial(_conv_out_proj_kernel, rows_per_seq=seq),
        grid=(m // tm,),
        in_specs=[
            pl.BlockSpec((tm, d), lambda i: (i, 0)),
            pl.BlockSpec((tm, aw), lambda i: (i, 0)),
            pl.BlockSpec((tm, bw), lambda i: (i, gate0)),
            pl.BlockSpec((tm, bw), lambda i: (i, gate0 + 1)),
            pl.BlockSpec((tm, bw), lambda i: (i, gate0 + 2)),
            prev_rows(gate0 + 1),
            prev_rows(gate0 + 2),
            pl.BlockSpec((CONV_WIDTH, bw), lambda i: (0, 0)),
            pl.BlockSpec(w_out.shape, lambda i: (0, 0), pipeline_mode=pl.Buffered(1)),
        ],
        out_specs=pl.BlockSpec((tm, d), lambda i: (i, 0)),
        out_shape=jax.ShapeDtypeStruct((m, d), x.dtype),
        scratch_shapes=[pltpu.VMEM((tm + halo, bw), jnp.float32),
                        pltpu.VMEM((tm, aw + bw), _BF16)],
        compiler_params=pltpu.CompilerParams(
            dimension_semantics=("parallel",),
            vmem_limit_bytes=_vmem_limit(pipelined, single, temps)),
        name="conv_out_proj",
    )(x, attn, proj, proj, proj, proj, proj, conv_w, w_out)


def _sgu_out_proj_kernel(x_ref, u_ref, v_ref, lg_ref, lb_ref, ws_ref, bs_ref, w_ref,
                         o_ref, vn_ref, lhs_ref):
    tm, cw = u_ref.shape
    gw = cw // C_GROUPS
    t_chunk = lax.broadcasted_iota(jnp.int32, (C_BLOCK, C_BLOCK), 0) // CHUNK
    s_chunk = lax.broadcasted_iota(jnp.int32, (C_BLOCK, C_BLOCK), 1) // CHUNK
    causal = s_chunk <= t_chunk
    w_m = [jnp.where(causal, ws_ref[g], 0.0).astype(vn_ref.dtype) for g in range(C_GROUPS)]

    half = tm // 2
    for r0 in range(0, tm, half):
        half_rows = slice(r0, r0 + half)
        v = v_ref[half_rows, :].astype(jnp.float32)
        mu = jnp.mean(v, axis=-1, keepdims=True)
        vc = v - mu
        var = jnp.mean(vc * vc, axis=-1, keepdims=True)
        vn_ref[half_rows, :] = (vc * lax.rsqrt(var + EPS) * lg_ref[...] + lb_ref[...]
                                ).astype(vn_ref.dtype)
        for g in range(C_GROUPS):
            bias = bs_ref[:, g:g + 1]
            cols = slice(g * gw, (g + 1) * gw)
            for n0 in range(r0, r0 + half, C_BLOCK):
                rows = slice(n0, n0 + C_BLOCK)
                s = jnp.dot(w_m[g], vn_ref[rows, cols],
                            preferred_element_type=jnp.float32) + bias
                lhs_ref[rows, cols] = (u_ref[rows, cols].astype(jnp.float32) * s
                                       ).astype(lhs_ref.dtype)
        o_ref[half_rows, :] = x_ref[half_rows, :] + jnp.dot(
            lhs_ref[half_rows, :], w_ref[...], preferred_element_type=jnp.float32)


def _sgu_out_proj(x, z, ln_g, ln_b, w_s, b_s, w_out):
    m, d = x.shape
    cw = z.shape[1] // 2
    tm = ROW_TILE_MIX
    assert m % tm == 0 and tm % (2 * C_BLOCK) == 0 and cw % C_GROUPS == 0
    pipelined = (2 * _nbytes((tm, d), x.dtype) + 2 * _nbytes((tm, cw), z.dtype)
                 + _nbytes(w_s.shape, w_s.dtype))
    single = _nbytes(w_out.shape, w_out.dtype) + 2 * _nbytes((tm, cw), _BF16)
    temps = 2 * _nbytes((tm // 2, cw), jnp.float32)
    return pl.pallas_call(
        _sgu_out_proj_kernel,
        grid=(m // tm,),
        in_specs=[
            pl.BlockSpec((tm, d), lambda i: (i, 0)),
            pl.BlockSpec((tm, cw), lambda i: (i, 0)),
            pl.BlockSpec((tm, cw), lambda i: (i, 1)),
            pl.BlockSpec((1, cw), lambda i: (0, 0)),
            pl.BlockSpec((1, cw), lambda i: (0, 0)),
            pl.BlockSpec(w_s.shape, lambda i: (0, 0, 0)),
            pl.BlockSpec((C_BLOCK, C_GROUPS), lambda i: (0, 0)),
            pl.BlockSpec(w_out.shape, lambda i: (0, 0), pipeline_mode=pl.Buffered(1)),
        ],
        out_specs=pl.BlockSpec((tm, d), lambda i: (i, 0)),
        out_shape=jax.ShapeDtypeStruct((m, d), x.dtype),
        scratch_shapes=[pltpu.VMEM((tm, cw), _BF16), pltpu.VMEM((tm, cw), _BF16)],
        compiler_params=pltpu.CompilerParams(
            dimension_semantics=("parallel",),
            vmem_limit_bytes=_vmem_limit(pipelined, single, temps)),
        name="sgu_out_proj",
    )(x, z, z, ln_g.reshape(1, cw), ln_b.reshape(1, cw), w_s, jnp.transpose(b_s), w_out)


def _ffn_kernel(x_hbm_ref, g_ref, wg_ref, wu_ref, wd_ref, fg_ref, o_ref, h_ref, x_ref,
                x_sem, *, final_norm):
    i, j = pl.program_id(0), pl.program_id(1)
    tm = x_ref.shape[0]
    tf = wg_ref.shape[1]

    def x_copy(tile):
        return pltpu.make_async_copy(x_hbm_ref.at[pl.ds(tile * tm, tm), :], x_ref, x_sem)

    @pl.when((i == 0) & (j == 0))
    def _():
        x_copy(0).start()

    @pl.when((j == 1) & (i + 1 < pl.num_programs(0)))
    def _():
        x_copy(i + 1).start()

    def swiglu_tile(h):
        acts = []
        for c0 in range(0, tf, tf // 2):
            cols = slice(c0, c0 + tf // 2)
            gate = jnp.dot(h, wg_ref[:, cols].astype(h.dtype),
                           preferred_element_type=jnp.float32)
            up = jnp.dot(h, wu_ref[:, cols].astype(h.dtype),
                         preferred_element_type=jnp.float32)
            acts.append((gate * jax.nn.sigmoid(gate) * up).astype(h.dtype))
        act = jnp.concatenate(acts, axis=1)
        return jnp.dot(act, wd_ref[...].astype(h.dtype), preferred_element_type=jnp.float32)

    @pl.when(j == 0)
    def _():
        x_copy(i).wait()
        x = x_ref[...]
        h = _rms_norm(x, g_ref[...]).astype(h_ref.dtype)
        h_ref[...] = h
        o_ref[...] = x + swiglu_tile(h)

    @pl.when(j > 0)
    def _():
        o_ref[...] += swiglu_tile(h_ref[...])

    if final_norm:
        @pl.when(j == pl.num_programs(1) - 1)
        def _():
            o_ref[...] = _rms_norm(o_ref[...], fg_ref[...])


def _ffn(x, g, w_gate, w_up, w_down, index, final_g, *, final_norm):
    m, d = x.shape
    f = w_gate.shape[2]
    tm, tf = ROW_TILE_PROJ, COL_TILE_FFN
    assert m % tm == 0 and f % tf == 0
    pipelined = _nbytes((tm, d), x.dtype) + 3 * _nbytes((d, tf), w_gate.dtype)
    single = _nbytes((tm, d), x.dtype) + _nbytes((tm, d), _BF16)
    temps = _nbytes((d, tf), _BF16) + _nbytes((tm, tf), jnp.float32)
    return pl.pallas_call(
        functools.partial(_ffn_kernel, final_norm=final_norm),
        grid=(m // tm, f // tf),
        in_specs=[
            pl.BlockSpec(memory_space=pl.ANY),
            pl.BlockSpec((1, d), lambda i, j: (0, 0)),
            pl.BlockSpec((None, d, tf), lambda i, j: (index, 0, j)),
            pl.BlockSpec((None, d, tf), lambda i, j: (index, 0, j)),
            pl.BlockSpec((None, tf, d), lambda i, j: (index, j, 0)),
            pl.BlockSpec((1, d), lambda i, j: (0, 0)),
        ],
        out_specs=pl.BlockSpec((tm, d), lambda i, j: (i, 0)),
        out_shape=jax.ShapeDtypeStruct((m, d), x.dtype),
        scratch_shapes=[pltpu.VMEM((tm, d), _BF16), pltpu.VMEM((tm, d), x.dtype),
                        pltpu.SemaphoreType.DMA(())],
        compiler_params=pltpu.CompilerParams(
            dimension_semantics=("arbitrary", "arbitrary"),
            vmem_limit_bytes=_vmem_limit(pipelined, single, temps)),
        name="ffn_final" if final_norm else "ffn",
    )(x, g.reshape(1, d), w_gate, w_up, w_down, final_g.reshape(1, d))


def kernel(x, mix_norm, ab_w_in, ab_rel_bias, ab_conv_w, ab_w_out, c_w_in, c_ln_g, c_ln_b,
           c_w_s, c_b_s, c_w_out, ffn_norm, ffn_w_gate, ffn_w_up, ffn_w_down, final_norm):
    batch, seq, d = x.shape
    depth = mix_norm.shape[0]
    heads = ab_rel_bias.shape[1]

    xf = x.reshape(batch * seq, d)
    bias, w_in = _bias_table(ab_rel_bias.reshape(-1, ab_rel_bias.shape[2]), ab_w_in, 0)
    w_out_sgu = None
    for layer in range(depth):
        i = layer // 2
        nxt = layer + 1
        next_w = None
        if nxt < depth:
            next_w = (c_w_in if nxt % 2 else ab_w_in, nxt // 2)
        if layer % 2 == 0:
            proj, w_in = _norm_proj(xf, mix_norm[layer], w_in, gelu=False, name="norm_proj",
                                    next_w=next_w)
            attn, w_out_conv, w_out_sgu = _attention(
                proj, bias, i * heads, ab_w_out, c_w_out, i, batch=batch, seq=seq, heads=heads)
            xf = _conv_out_proj(xf, attn, proj, ab_conv_w[i], w_out_conv, seq=seq)
        else:
            z, w_in = _norm_proj(xf, mix_norm[layer], w_in, gelu=True, name="norm_proj_gelu",
                                 next_w=next_w)
            xf = _sgu_out_proj(xf, z, c_ln_g[i], c_ln_b[i], c_w_s[i], c_b_s[i], w_out_sgu)
        xf = _ffn(xf, ffn_norm[layer], ffn_w_gate, ffn_w_up, ffn_w_down, layer, final_norm,
                  final_norm=(layer == depth - 1))
    return xf.reshape(batch, seq, d)
```

```python
import functools

import jax
import jax.numpy as jnp
import numpy as np
from jax import lax
from jax.experimental import pallas as pl
from jax.experimental.pallas import tpu as pltpu

CHUNK = 64
A_HEAD_DIM = 128
A_LEFT_CHUNKS = 8
A_MAX_REL = 256
CONV_WIDTH = 3
C_BLOCK = 128
C_GROUPS = 8
EPS = 1e-6
NEG_INF = -1e30

V7X_VMEM_BYTES = 64 * 1024 * 1024
V7X_BF16_SUBLANES = 16
VMEM_CAP_BYTES = V7X_VMEM_BYTES - 2 * 1024 * 1024
VMEM_SLACK_DIV = 8

ROW_TILE_PROJ = 1024
ROW_TILE_MIX = 512
COL_TILE_PROJ = 2048
COL_TILE_FFN = 512
Q_BLOCK = 2 * CHUNK
K_WINDOW = Q_BLOCK + A_LEFT_CHUNKS * CHUNK
ATTN_INTERLEAVE = 32
ATTN_HEADS = 4

_SQRT_HALF = np.sqrt(0.5).astype(np.float32)
_BF16 = jnp.bfloat16


def _vmem_limit(pipelined_bytes, single_bytes, temp_bytes):
    need = 2 * pipelined_bytes + single_bytes + temp_bytes
    return int(min(need + need // VMEM_SLACK_DIV, VMEM_CAP_BYTES))


def _nbytes(shape, dtype):
    return int(np.prod(shape)) * jnp.dtype(dtype).itemsize


def _rms_norm(x, g):
    ms = jnp.mean(x * x, axis=-1, keepdims=True)
    return (x * lax.rsqrt(ms + EPS)) * g


def _norm_proj_kernel(*refs, gelu, has_next_w):
    if has_next_w:
        x_ref, g_ref, w_ref, next_w_ref, o_ref, next_w16_ref, h_ref = refs
    else:
        x_ref, g_ref, w_ref, o_ref, h_ref = refs

    def project(h):
        if has_next_w:
            next_w16_ref[...] = next_w_ref[...].astype(next_w16_ref.dtype)
        y = jnp.dot(h, w_ref[...], preferred_element_type=jnp.float32)
        if gelu:
            y = 0.5 * y * (1.0 + lax.erf(y * _SQRT_HALF))
        o_ref[...] = y.astype(o_ref.dtype)

    j = pl.program_id(1)

    @pl.when(j == 0)
    def _():
        h = _rms_norm(x_ref[...], g_ref[...]).astype(h_ref.dtype)
        h_ref[...] = h
        project(h)

    @pl.when(j > 0)
    def _():
        project(h_ref[...])


def _norm_proj(x, g, w, *, gelu, name, next_w=None):
    m, d = x.shape
    n = w.shape[1]
    tm, tn = ROW_TILE_PROJ, COL_TILE_PROJ
    assert m % tm == 0 and n % tn == 0
    grid = (m // tm, n // tn)
    in_specs = [
        pl.BlockSpec((tm, d), lambda i, j: (i, 0)),
        pl.BlockSpec((1, d), lambda i, j: (0, 0)),
        pl.BlockSpec((d, tn), lambda i, j: (0, j)),
    ]
    out_specs = [pl.BlockSpec((tm, tn), lambda i, j: (i, j))]
    out_shape = [jax.ShapeDtypeStruct((m, n), _BF16)]
    operands = [x, g.reshape(1, d), w]
    cast_bytes = 0
    if next_w is not None:
        stacked, index = next_w
        _, r, c = stacked.shape
        n_steps = grid[0] * grid[1]
        n_blocks = max(k for k in range(1, n_steps + 1) if r % (k * V7X_BF16_SUBLANES) == 0)
        rows = r // n_blocks

        def block_of(i, j):
            return jnp.minimum(i * grid[1] + j, n_blocks - 1)

        in_specs.append(pl.BlockSpec((None, rows, c), lambda i, j: (index, block_of(i, j), 0)))
        out_specs.append(pl.BlockSpec((rows, c), lambda i, j: (block_of(i, j), 0)))
        out_shape.append(jax.ShapeDtypeStruct((r, c), _BF16))
        operands.append(stacked)
        cast_bytes = _nbytes((rows, c), stacked.dtype) + _nbytes((rows, c), _BF16)
    pipelined = (_nbytes((tm, d), x.dtype) + _nbytes((d, tn), w.dtype)
                 + _nbytes((tm, tn), _BF16) + cast_bytes)
    scratch = _nbytes((tm, d), _BF16)
    temps = _nbytes((tm, tn), jnp.float32)
    outs = pl.pallas_call(
        functools.partial(_norm_proj_kernel, gelu=gelu, has_next_w=next_w is not None),
        grid=grid,
        in_specs=in_specs,
        out_specs=out_specs,
        out_shape=out_shape,
        scratch_shapes=[pltpu.VMEM((tm, d), _BF16)],
        compiler_params=pltpu.CompilerParams(
            dimension_semantics=("arbitrary", "arbitrary"),
            vmem_limit_bytes=_vmem_limit(pipelined, scratch, temps)),
        name=name,
    )(*operands)
    return outs[0], (outs[1] if next_w is not None else None)


def _bias_table_kernel(g_ref, w_ref, o_ref, w16_ref):
    w16_ref[...] = w_ref[...].astype(w16_ref.dtype)
    width = Q_BLOCK + K_WINDOW
    x = jnp.broadcast_to(g_ref[0], (Q_BLOCK, width))
    row = lax.broadcasted_iota(jnp.int32, (Q_BLOCK, width), 0)
    shift = 1
    while shift < Q_BLOCK:
        x = jnp.where((row & shift) != 0, pltpu.roll(x, shift, 1), x)
        shift *= 2
    t = x[:, Q_BLOCK:]
    q_chunk = lax.broadcasted_iota(jnp.int32, (Q_BLOCK, K_WINDOW), 0) // CHUNK
    k_chunk = lax.broadcasted_iota(jnp.int32, (Q_BLOCK, K_WINDOW), 1) // CHUNK
    band = k_chunk - q_chunk
    o_ref[0] = jnp.where((band >= 0) & (band <= A_LEFT_CHUNKS), t, NEG_INF)


def _bias_table(rel_bias, first_w, index):
    heads, n_rel = rel_bias.shape
    width = Q_BLOCK + K_WINDOW
    _, r, c = first_w.shape
    assert r % (heads * V7X_BF16_SUBLANES) == 0
    rows = r // heads
    left = Q_BLOCK + A_LEFT_CHUNKS * CHUNK - A_MAX_REL
    right = max(0, width - left - n_rel)
    g = jnp.pad(rel_bias[:, ::-1], ((0, 0), (left, right)), mode="edge")[:, :width]
    pipelined = (_nbytes((Q_BLOCK, K_WINDOW), jnp.float32)
                 + _nbytes((rows, c), first_w.dtype) + _nbytes((rows, c), _BF16))
    return pl.pallas_call(
        _bias_table_kernel,
        grid=(heads,),
        in_specs=[pl.BlockSpec((1, 1, width), lambda h: (h, 0, 0)),
                  pl.BlockSpec((None, rows, c), lambda h: (index, h, 0))],
        out_specs=[pl.BlockSpec((1, Q_BLOCK, K_WINDOW), lambda h: (h, 0, 0)),
                   pl.BlockSpec((rows, c), lambda h: (h, 0))],
        out_shape=[jax.ShapeDtypeStruct((heads, Q_BLOCK, K_WINDOW), jnp.float32),
                   jax.ShapeDtypeStruct((r, c), _BF16)],
        compiler_params=pltpu.CompilerParams(
            dimension_semantics=("parallel",),
            vmem_limit_bytes=_vmem_limit(
                pipelined, _nbytes((r, c), _BF16),
                6 * _nbytes((Q_BLOCK, width), jnp.float32))),
        name="bias_table",
    )(g.reshape(heads, 1, width), first_w)


def _attention_kernel(q_ref, k_ref, v_ref, b_ref, wa_ref, wc_ref, o_ref, wa16_ref, wc16_ref,
                      kt_ref, v1_ref):
    wa16_ref[...] = wa_ref[...].astype(wa16_ref.dtype)
    wc16_ref[...] = wc_ref[...].astype(wc16_ref.dtype)
    seq = q_ref.shape[0]
    dh = A_HEAD_DIM
    kt_ref[...] = k_ref[...].T
    blocks = []
    for hd in range(ATTN_HEADS):
        cols = slice(hd * dh, (hd + 1) * dh)
        v1_ref[hd, :, :dh] = v_ref[:, cols]
        v1_ref[hd, :, dh:] = jnp.ones((seq, dh), v1_ref.dtype)
        for q0 in range(0, seq, Q_BLOCK):
            blocks.append((hd, cols, q0, max(0, q0 + Q_BLOCK - K_WINDOW), q0 + Q_BLOCK))
    for g0 in range(0, len(blocks), ATTN_INTERLEAVE):
        group = blocks[g0:g0 + ATTN_INTERLEAVE]
        s = [jnp.dot(q_ref[q0:k1, cols], kt_ref[cols, k0:k1],
                     preferred_element_type=jnp.float32) for hd, cols, q0, k0, k1 in group]
        s = [si * (dh ** -0.5) + b_ref[hd, :, K_WINDOW - (k1 - k0):]
             for si, (hd, cols, q0, k0, k1) in zip(s, group)]
        m = [jnp.max(si, axis=-1, keepdims=True) for si in s]
        p = [jnp.exp(si - mi).astype(v1_ref.dtype) for si, mi in zip(s, m)]
        o = [jnp.dot(pi, v1_ref[hd, k0:k1, :], preferred_element_type=jnp.float32)
             for pi, (hd, cols, q0, k0, k1) in zip(p, group)]
        for oi, (hd, cols, q0, k0, k1) in zip(o, group):
            o_ref[q0:k1, cols] = (oi[:, :dh] * (1.0 / oi[:, dh:dh + 1])).astype(o_ref.dtype)


def _attention(proj, bias, table0, w_a, w_c, index, *, batch, seq, heads):
    dh = A_HEAD_DIM
    hs = ATTN_HEADS
    assert seq % Q_BLOCK == 0 and K_WINDOW % Q_BLOCK == 0
    assert (hs * seq // Q_BLOCK) % ATTN_INTERLEAVE == 0 and heads % hs == 0 and table0 % hs == 0
    n_hb = heads // hs
    n_steps = batch * n_hb
    _, r, c = w_a.shape
    assert w_c.shape[1:] == (r, c) and r % (n_steps * V7X_BF16_SUBLANES) == 0
    rows = r // n_steps
    w_in_spec = pl.BlockSpec((None, rows, c), lambda b, h: (index, b * n_hb + h, 0))
    w_out_spec = pl.BlockSpec((rows, c), lambda b, h: (b * n_hb + h, 0))
    pipelined = (4 * _nbytes((seq, hs * dh), proj.dtype)
                 + _nbytes((hs, Q_BLOCK, K_WINDOW), jnp.float32)
                 + 2 * (_nbytes((rows, c), w_a.dtype) + _nbytes((rows, c), _BF16)))
    scratch = 3 * _nbytes((seq, hs * dh), proj.dtype)
    temps = 2 * ATTN_INTERLEAVE * _nbytes((Q_BLOCK, K_WINDOW), jnp.float32)
    return pl.pallas_call(
        _attention_kernel,
        grid=(batch, n_hb),
        in_specs=[
            pl.BlockSpec((seq, hs * dh), lambda b, h: (b, h)),
            pl.BlockSpec((seq, hs * dh), lambda b, h: (b, n_hb + h)),
            pl.BlockSpec((seq, hs * dh), lambda b, h: (b, 2 * n_hb + h)),
            pl.BlockSpec((hs, Q_BLOCK, K_WINDOW), lambda b, h: (table0 // hs + h, 0, 0)),
            w_in_spec,
            w_in_spec,
        ],
        out_specs=[pl.BlockSpec((seq, hs * dh), lambda b, h: (b, h)), w_out_spec, w_out_spec],
        out_shape=[jax.ShapeDtypeStruct((batch * seq, heads * dh), proj.dtype),
                   jax.ShapeDtypeStruct((r, c), _BF16), jax.ShapeDtypeStruct((r, c), _BF16)],
        scratch_shapes=[pltpu.VMEM((hs * dh, seq), proj.dtype),
                        pltpu.VMEM((hs, seq, 2 * dh), proj.dtype)],
        compiler_params=pltpu.CompilerParams(
            dimension_semantics=("parallel", "parallel"),
            vmem_limit_bytes=_vmem_limit(pipelined, scratch, temps)),
        name="band_attention",
    )(proj, proj, proj, bias, w_a, w_c)


def _conv_out_proj_kernel(x_ref, a_ref, bg_ref, cg_ref, hv_ref, cprev_ref, hprev_ref,
                          cw_ref, w_ref, o_ref, z_ref, lhs_ref, *, rows_per_seq):
    tm, aw = a_ref.shape
    halo = cprev_ref.shape[0]
    f32 = jnp.float32
    i = pl.program_id(0)
    at_seq_start = (i * tm) % rows_per_seq == 0
    zprev = cprev_ref[...].astype(f32) * hprev_ref[...].astype(f32)
    z_ref[:halo] = jnp.where(at_seq_start, 0.0, zprev)
    z_ref[halo:] = cg_ref[...].astype(f32) * hv_ref[...].astype(f32)
    y = cw_ref[0:1] * z_ref[pl.ds(halo - 2, tm)]
    y = y + cw_ref[1:2] * z_ref[pl.ds(halo - 1, tm)]
    y = y + cw_ref[2:3] * z_ref[pl.ds(halo, tm)]
    lhs_ref[:, :aw] = a_ref[...]
    lhs_ref[:, aw:] = (bg_ref[...].astype(f32) * y).astype(lhs_ref.dtype)
    o_ref[...] = x_ref[...] + jnp.dot(lhs_ref[...], w_ref[...], preferred_element_type=f32)


def _conv_out_proj(x, attn, proj, conv_w, w_out, *, seq):
    m, d = x.shape
    aw = attn.shape[1]
    bw = conv_w.shape[1]
    tm = ROW_TILE_MIX
    halo = V7X_BF16_SUBLANES
    assert m % tm == 0 and seq % tm == 0 and CONV_WIDTH - 1 <= halo
    assert conv_w.shape == (CONV_WIDTH, bw) and CONV_WIDTH == 3
    gate0 = (proj.shape[1] - 3 * bw) // bw
    assert gate0 * bw + 3 * bw == proj.shape[1]
    halo_blocks = tm // halo

    def prev_rows(col):
        return pl.BlockSpec(
            (halo, bw), lambda i: (jnp.maximum(i * halo_blocks - 1, 0), col))

    pipelined = (2 * _nbytes((tm, d), x.dtype) + _nbytes((tm, aw), attn.dtype)
                 + 3 * _nbytes((tm, bw), proj.dtype))
    single = (_nbytes(w_out.shape, w_out.dtype) + _nbytes((tm + halo, bw), jnp.float32)
              + _nbytes((tm, aw + bw), _BF16))
    temps = 2 * _nbytes((tm, bw), jnp.float32)
    return pl.pallas_call(
        functools.partial(_conv_out_proj_kernel, rows_per_seq=seq),
        grid=(m // tm,),
        in_specs=[
            pl.BlockSpec((tm, d), lambda i: (i, 0)),
            pl.BlockSpec((tm, aw), lambda i: (i, 0)),
            pl.BlockSpec((tm, bw), lambda i: (i, gate0)),
            pl.BlockSpec((tm, bw), lambda i: (i, gate0 + 1)),
            pl.BlockSpec((tm, bw), lambda i: (i, gate0 + 2)),
            prev_rows(gate0 + 1),
            prev_rows(gate0 + 2),
            pl.BlockSpec((CONV_WIDTH, bw), lambda i: (0, 0)),
            pl.BlockSpec(w_out.shape, lambda i: (0, 0), pipeline_mode=pl.Buffered(1)),
        ],
        out_specs=pl.BlockSpec((tm, d), lambda i: (i, 0)),
        out_shape=jax.ShapeDtypeStruct((m, d), x.dtype),
        scratch_shapes=[pltpu.VMEM((tm + halo, bw), jnp.float32),
                        pltpu.VMEM((tm, aw + bw), _BF16)],
        compiler_params=pltpu.CompilerParams(
            dimension_semantics=("parallel",),
            vmem_limit_bytes=_vmem_limit(pipelined, single, temps)),
        name="conv_out_proj",
    )(x, attn, proj, proj, proj, proj, proj, conv_w, w_out)


def _sgu_out_proj_kernel(*refs, n_side):
    x_ref, u_ref, v_ref, lg_ref, lb_ref, ws_ref, bs_ref, w_ref = refs[:8]
    side_in = refs[8:8 + n_side]
    o_ref = refs[8 + n_side]
    side_out = refs[9 + n_side:9 + 2 * n_side]
    vn_ref, lhs_ref = refs[9 + 2 * n_side:]
    for src, dst in zip(side_in, side_out):
        dst[...] = src[...].astype(dst.dtype)
    tm, cw = u_ref.shape
    gw = cw // C_GROUPS
    t_chunk = lax.broadcasted_iota(jnp.int32, (C_BLOCK, C_BLOCK), 0) // CHUNK
    s_chunk = lax.broadcasted_iota(jnp.int32, (C_BLOCK, C_BLOCK), 1) // CHUNK
    causal = s_chunk <= t_chunk
    w_m = [jnp.where(causal, ws_ref[g], 0.0).astype(vn_ref.dtype) for g in range(C_GROUPS)]

    half = tm // 2
    for r0 in range(0, tm, half):
        half_rows = slice(r0, r0 + half)
        v = v_ref[half_rows, :].astype(jnp.float32)
        mu = jnp.mean(v, axis=-1, keepdims=True)
        vc = v - mu
        var = jnp.mean(vc * vc, axis=-1, keepdims=True)
        vn_ref[half_rows, :] = (vc * lax.rsqrt(var + EPS) * lg_ref[...] + lb_ref[...]
                                ).astype(vn_ref.dtype)
        for g in range(C_GROUPS):
            bias = bs_ref[:, g:g + 1]
            cols = slice(g * gw, (g + 1) * gw)
            for n0 in range(r0, r0 + half, C_BLOCK):
                rows = slice(n0, n0 + C_BLOCK)
                s = jnp.dot(w_m[g], vn_ref[rows, cols],
                            preferred_element_type=jnp.float32) + bias
                lhs_ref[rows, cols] = (u_ref[rows, cols].astype(jnp.float32) * s
                                       ).astype(lhs_ref.dtype)
        o_ref[half_rows, :] = x_ref[half_rows, :] + jnp.dot(
            lhs_ref[half_rows, :], w_ref[...], preferred_element_type=jnp.float32)


def _sgu_out_proj(x, z, ln_g, ln_b, w_s, b_s, w_out, side=()):
    m, d = x.shape
    cw = z.shape[1] // 2
    tm = ROW_TILE_MIX
    assert m % tm == 0 and tm % (2 * C_BLOCK) == 0 and cw % C_GROUPS == 0
    grid = (m // tm,)
    side_in, side_out, side_shapes, side_bytes = _side_job_specs(side, grid)
    pipelined = (2 * _nbytes((tm, d), x.dtype) + 2 * _nbytes((tm, cw), z.dtype)
                 + _nbytes(w_s.shape, w_s.dtype) + side_bytes)
    single = _nbytes(w_out.shape, w_out.dtype) + 2 * _nbytes((tm, cw), _BF16)
    temps = 2 * _nbytes((tm // 2, cw), jnp.float32)
    outs = pl.pallas_call(
        functools.partial(_sgu_out_proj_kernel, n_side=len(side)),
        grid=grid,
        in_specs=[
            pl.BlockSpec((tm, d), lambda i: (i, 0)),
            pl.BlockSpec((tm, cw), lambda i: (i, 0)),
            pl.BlockSpec((tm, cw), lambda i: (i, 1)),
            pl.BlockSpec((1, cw), lambda i: (0, 0)),
            pl.BlockSpec((1, cw), lambda i: (0, 0)),
            pl.BlockSpec(w_s.shape, lambda i: (0, 0, 0)),
            pl.BlockSpec((C_BLOCK, C_GROUPS), lambda i: (0, 0)),
            pl.BlockSpec(w_out.shape, lambda i: (0, 0), pipeline_mode=pl.Buffered(1)),
        ] + side_in,
        out_specs=[pl.BlockSpec((tm, d), lambda i: (i, 0))] + side_out,
        out_shape=[jax.ShapeDtypeStruct((m, d), x.dtype)] + side_shapes,
        scratch_shapes=[pltpu.VMEM((tm, cw), _BF16), pltpu.VMEM((tm, cw), _BF16)],
        compiler_params=pltpu.CompilerParams(
            dimension_semantics=("parallel",),
            vmem_limit_bytes=_vmem_limit(pipelined, single, temps)),
        name="sgu_out_proj",
    )(x, z, z, ln_g.reshape(1, cw), ln_b.reshape(1, cw), w_s, jnp.transpose(b_s), w_out,
      *[stacked for stacked, _ in side])
    return outs[0], list(outs[1:])


def _ffn_kernel(*refs, final_norm, n_side):
    x_hbm_ref, g_ref, wg_ref, wu_ref, wd_ref, fg_ref = refs[:6]
    side_in = refs[6:6 + n_side]
    o_ref = refs[6 + n_side]
    side_out = refs[7 + n_side:7 + 2 * n_side]
    h_ref, x_ref, x_sem = refs[7 + 2 * n_side:]
    i, j = pl.program_id(0), pl.program_id(1)
    tm = x_ref.shape[0]
    tf = wg_ref.shape[1]

    def x_copy(tile):
        return pltpu.make_async_copy(x_hbm_ref.at[pl.ds(tile * tm, tm), :], x_ref, x_sem)

    @pl.when((i == 0) & (j == 0))
    def _():
        x_copy(0).start()

    @pl.when((j == 1) & (i + 1 < pl.num_programs(0)))
    def _():
        x_copy(i + 1).start()

    def swiglu_tile(h):
        for src, dst in zip(side_in, side_out):
            dst[...] = src[...].astype(dst.dtype)
        acts = []
        for c0 in range(0, tf, tf // 2):
            cols = slice(c0, c0 + tf // 2)
            gate = jnp.dot(h, wg_ref[:, cols].astype(h.dtype),
                           preferred_element_type=jnp.float32)
            up = jnp.dot(h, wu_ref[:, cols].astype(h.dtype),
                         preferred_element_type=jnp.float32)
            acts.append((gate * jax.nn.sigmoid(gate) * up).astype(h.dtype))
        act = jnp.concatenate(acts, axis=1)
        return jnp.dot(act, wd_ref[...].astype(h.dtype), preferred_element_type=jnp.float32)

    @pl.when(j == 0)
    def _():
        x_copy(i).wait()
        x = x_ref[...]
        h = _rms_norm(x, g_ref[...]).astype(h_ref.dtype)
        h_ref[...] = h
        o_ref[...] = x + swiglu_tile(h)

    @pl.when(j > 0)
    def _():
        o_ref[...] += swiglu_tile(h_ref[...])

    if final_norm:
        @pl.when(j == pl.num_programs(1) - 1)
        def _():
            o_ref[...] = _rms_norm(o_ref[...], fg_ref[...])


def _side_job_specs(side, grid):
    n_steps = int(np.prod(grid))
    strides = [int(np.prod(grid[a + 1:])) for a in range(len(grid))]
    in_specs, out_specs, shapes, nbytes = [], [], [], 0
    for stacked, index in side:
        _, r, c = stacked.shape
        n_blocks = max(k for k in range(1, n_steps + 1) if r % (k * V7X_BF16_SUBLANES) == 0)
        rows = r // n_blocks

        def block_of(*ids, n_blocks=n_blocks):
            step = sum(i * s for i, s in zip(ids, strides))
            return jnp.minimum(step, n_blocks - 1)

        in_specs.append(pl.BlockSpec(
            (None, rows, c),
            lambda *ids, index=index, block_of=block_of: (index, block_of(*ids), 0)))
        out_specs.append(pl.BlockSpec(
            (rows, c), lambda *ids, block_of=block_of: (block_of(*ids), 0)))
        shapes.append(jax.ShapeDtypeStruct((r, c), _BF16))
        nbytes += _nbytes((rows, c), stacked.dtype) + _nbytes((rows, c), _BF16)
    return in_specs, out_specs, shapes, nbytes


def _ffn(x, g, w_gate, w_up, w_down, index, final_g, *, final_norm, side=()):
    m, d = x.shape
    f = w_gate.shape[-1]
    tm, tf = ROW_TILE_PROJ, COL_TILE_FFN
    assert m % tm == 0 and f % tf == 0
    grid = (m // tm, f // tf)
    if index is None:
        w_specs = [pl.BlockSpec((d, tf), lambda i, j: (0, j)),
                   pl.BlockSpec((d, tf), lambda i, j: (0, j)),
                   pl.BlockSpec((tf, d), lambda i, j: (j, 0))]
    else:
        w_specs = [pl.BlockSpec((None, d, tf), lambda i, j: (index, 0, j)),
                   pl.BlockSpec((None, d, tf), lambda i, j: (index, 0, j)),
                   pl.BlockSpec((None, tf, d), lambda i, j: (index, j, 0))]
    side_in, side_out, side_shapes, side_bytes = _side_job_specs(side, grid)
    pipelined = (_nbytes((tm, d), x.dtype) + 3 * _nbytes((d, tf), w_gate.dtype) + side_bytes)
    single = _nbytes((tm, d), x.dtype) + _nbytes((tm, d), _BF16)
    temps = _nbytes((d, tf), _BF16) + _nbytes((tm, tf), jnp.float32)
    outs = pl.pallas_call(
        functools.partial(_ffn_kernel, final_norm=final_norm, n_side=len(side)),
        grid=grid,
        in_specs=[pl.BlockSpec(memory_space=pl.ANY),
                  pl.BlockSpec((1, d), lambda i, j: (0, 0))] + w_specs
                 + [pl.BlockSpec((1, d), lambda i, j: (0, 0))] + side_in,
        out_specs=[pl.BlockSpec((tm, d), lambda i, j: (i, 0))] + side_out,
        out_shape=[jax.ShapeDtypeStruct((m, d), x.dtype)] + side_shapes,
        scratch_shapes=[pltpu.VMEM((tm, d), _BF16), pltpu.VMEM((tm, d), x.dtype),
                        pltpu.SemaphoreType.DMA(())],
        compiler_params=pltpu.CompilerParams(
            dimension_semantics=("arbitrary", "arbitrary"),
            vmem_limit_bytes=_vmem_limit(pipelined, single, temps)),
        name="ffn_final" if final_norm else "ffn",
    )(x, g.reshape(1, d), w_gate, w_up, w_down, final_g.reshape(1, d),
      *[stacked for stacked, _ in side])
    return outs[0], list(outs[1:])


def kernel(x, mix_norm, ab_w_in, ab_rel_bias, ab_conv_w, ab_w_out, c_w_in, c_ln_g, c_ln_b,
           c_w_s, c_b_s, c_w_out, ffn_norm, ffn_w_gate, ffn_w_up, ffn_w_down, final_norm):
    batch, seq, d = x.shape
    depth = mix_norm.shape[0]
    heads = ab_rel_bias.shape[1]

    xf = x.reshape(batch * seq, d)
    bias, w_in = _bias_table(ab_rel_bias.reshape(-1, ab_rel_bias.shape[2]), ab_w_in, 0)
    w_out_sgu = None
    ffn_f32 = (ffn_w_gate, ffn_w_up, ffn_w_down)
    w_ffn, ffn_index = list(ffn_f32), 0
    for layer in range(depth):
        i = layer // 2
        nxt = layer + 1
        next_w = None
        if nxt < depth:
            next_w = (c_w_in if nxt % 2 else ab_w_in, nxt // 2)
        if layer % 2 == 0:
            proj, w_in = _norm_proj(xf, mix_norm[layer], w_in, gelu=False, name="norm_proj",
                                    next_w=next_w)
            attn, w_out_conv, w_out_sgu = _attention(
                proj, bias, i * heads, ab_w_out, c_w_out, i, batch=batch, seq=seq, heads=heads)
            xf = _conv_out_proj(xf, attn, proj, ab_conv_w[i], w_out_conv, seq=seq)
        else:
            z, w_in = _norm_proj(xf, mix_norm[layer], w_in, gelu=True, name="norm_proj_gelu",
                                 next_w=next_w)
            side = [(ffn_w_down, layer)] if layer == 1 else []
            xf, cast = _sgu_out_proj(xf, z, c_ln_g[i], c_ln_b[i], c_w_s[i], c_b_s[i],
                                     w_out_sgu, side=side)
            if cast:
                w_ffn = w_ffn[:2] + cast
        if nxt >= depth:
            side = []
        elif layer == 0:
            side = [(w, nxt) for w in ffn_f32[:2]]
        else:
            side = [(w, nxt) for w in ffn_f32]
        xf, cast = _ffn(xf, ffn_norm[layer], *w_ffn, ffn_index, final_norm,
                        final_norm=(nxt == depth), side=side)
        w_ffn, ffn_index = cast + [None] * (3 - len(cast)), None
    return xf.reshape(batch, seq, d)
```
